```python
import math
import jax, jax.numpy as jnp
from jax import lax
import numpy as np

D_MODEL = 2048
BATCH = 2
SEQ = 16384
DEPTH = 1

HEAD_DIM = 128
MOBA_HEADS = D_MODEL // (2 * HEAD_DIM)
MOBA_BLOCK = 256
MOBA_TOPK = 3
MOBA_Q_CHUNK = 64
DIFF_HEADS = D_MODEL // (4 * HEAD_DIM)
DIFF_V_DIM = 2 * HEAD_DIM
DIFF_Q_BLOCK = 128
ROT_DIM = HEAD_DIM // 4
ROPE_THETA = 500000.0
D_FF = 4 * D_MODEL
PLE_DIM = 256
NORM_EPS = 1e-6
MOBA_WIDTH = MOBA_HEADS * HEAD_DIM
DIFF_QK_WIDTH = DIFF_HEADS * 2 * HEAD_DIM
DIFF_V_WIDTH = DIFF_HEADS * DIFF_V_DIM
IN_WIDTH = 3 * MOBA_WIDTH + 2 * DIFF_QK_WIDTH + DIFF_V_WIDTH + 2 * D_MODEL

kernel_name = "hybrid_moba_diffattn_gated_block"

F32 = jnp.float32


def rmsnorm(x, g):
    xf = x.astype(F32)
    y = xf * lax.rsqrt(jnp.mean(xf * xf, axis=-1, keepdims=True) + NORM_EPS)
    return (y * g.astype(F32)).astype(x.dtype)


def partial_rope(x, pos_f):
    half = ROT_DIM // 2
    inv_freq = 1.0 / (ROPE_THETA ** (jnp.arange(half, dtype=F32) * 2.0 / ROT_DIM))
    ang = pos_f[:, None] * inv_freq[None, :]
    cos, sin = jnp.cos(ang), jnp.sin(ang)
    xf = x.astype(F32)
    x1 = xf[..., :half]
    x2 = xf[..., half:ROT_DIM]
    out = jnp.concatenate([x1 * cos - x2 * sin, x2 * cos + x1 * sin, xf[..., ROT_DIM:]], axis=-1)
    return out.astype(x.dtype)


def moba_attention(q, k, v):
    B, H, S, Dh = q.shape
    nb = -(-S // MOBA_BLOCK)
    topk = min(MOBA_TOPK, nb)
    pad = nb * MOBA_BLOCK - S
    k_p = jnp.pad(k, ((0, 0), (0, 0), (0, pad), (0, 0)))
    v_p = jnp.pad(v, ((0, 0), (0, 0), (0, pad), (0, 0)))
    k_blk = k_p.reshape(B, H, nb, MOBA_BLOCK, Dh)
    v_blk = v_p.reshape(B, H, nb, MOBA_BLOCK, Dh)
    k_mean = jnp.mean(k_blk.astype(F32), axis=3)
    nc = S // MOBA_Q_CHUNK
    q_ch = jnp.moveaxis(q.reshape(B, H, nc, MOBA_Q_CHUNK, Dh), 2, 0)
    bi = jnp.arange(B)[:, None, None, None]
    hi = jnp.arange(H)[None, :, None, None]
    blk_ids = jnp.arange(nb)
    scale = Dh ** -0.5

    def chunk(args):
        ci, qc = args
        q_start = ci * MOBA_Q_CHUNK
        cur = q_start // MOBA_BLOCK
        gate = jnp.einsum('bhqd,bhnd->bhqn', qc.astype(F32), k_mean)
        gate = jnp.where(blk_ids < cur, gate, -jnp.inf)
        _, idx = lax.top_k(gate, topk)
        valid = idx < cur
        k_sel = k_blk[bi, hi, idx]
        v_sel = v_blk[bi, hi, idx]
        s_sel = jnp.einsum('bhqd,bhqjkd->bhqjk', qc, k_sel).astype(F32) * scale
        s_sel = jnp.where(valid[..., None], s_sel, -jnp.inf)
        s_sel = s_sel.reshape(B, H, MOBA_Q_CHUNK, topk * MOBA_BLOCK)
        own0 = cur * MOBA_BLOCK
        k_own = lax.dynamic_slice_in_dim(k_p, own0, MOBA_BLOCK, axis=2)
        v_own = lax.dynamic_slice_in_dim(v_p, own0, MOBA_BLOCK, axis=2)
        s_own = jnp.einsum('bhqd,bhkd->bhqk', qc, k_own).astype(F32) * scale
        q_pos = q_start + jnp.arange(MOBA_Q_CHUNK)
        k_pos = own0 + jnp.arange(MOBA_BLOCK)
        s_own = jnp.where(k_pos[None, :] <= q_pos[:, None], s_own, -jnp.inf)
        probs = jax.nn.softmax(jnp.concatenate([s_sel, s_own], axis=-1), axis=-1).astype(v.dtype)
        p_sel = probs[..., :topk * MOBA_BLOCK].reshape(B, H, MOBA_Q_CHUNK, topk, MOBA_BLOCK)
        p_own = probs[..., topk * MOBA_BLOCK:]
        return (jnp.einsum('bhqjk,bhqjkd->bhqd', p_sel, v_sel)
                + jnp.einsum('bhqk,bhkd->bhqd', p_own, v_own))

    out = lax.map(chunk, (jnp.arange(nc), q_ch))
    return jnp.moveaxis(out, 0, 2).reshape(B, H, S, Dh)


def diff_attention(q, k, v, lam, sub_g, lambda_init):
    B, H, _, S, Dh = q.shape
    nq = S // DIFF_Q_BLOCK
    q_bl = jnp.moveaxis(q.reshape(B, H, 2, nq, DIFF_Q_BLOCK, Dh), 3, 0)
    k_pos = jnp.arange(S)
    scale = Dh ** -0.5

    def block(args):
        bidx, qb = args
        s = jnp.einsum('bhcqd,bhckd->bhcqk', qb, k).astype(F32) * scale
        q_pos = bidx * DIFF_Q_BLOCK + jnp.arange(DIFF_Q_BLOCK)
        s = jnp.where(k_pos[None, :] <= q_pos[:, None], s, -jnp.inf)
        pr = jax.nn.softmax(s, axis=-1)
        attn = (pr[:, :, 0] - lam * pr[:, :, 1]).astype(v.dtype)
        return jnp.einsum('bhqk,bhkd->bhqd', attn, v)

    out = lax.map(block, (jnp.arange(nq), q_bl))
    out = jnp.moveaxis(out, 0, 2).reshape(B, H, S, DIFF_V_DIM)
    return (rmsnorm(out, sub_g) * (1.0 - lambda_init)).astype(v.dtype)


def setup_inputs(seed: int = 0) -> dict:
    key = jax.random.key(seed)
    ks = jax.random.split(key, 24)
    n = lambda k_, shape, s: jax.random.normal(k_, shape, F32) * s
    gain = lambda k_, dim: 1.0 + 0.02 * jax.random.normal(k_, (DEPTH, dim), F32)
    return {
        "x": n(ks[0], (BATCH, SEQ, D_MODEL), 1.0),
        "p": n(ks[1], (DEPTH, BATCH, SEQ, PLE_DIM), 1.0),
        "w_in": n(ks[2], (DEPTH, D_MODEL, IN_WIDTH), D_MODEL ** -0.5),
        "w_br_moba": n(ks[3], (DEPTH, MOBA_WIDTH, D_MODEL), MOBA_WIDTH ** -0.5),
        "w_br_diff": n(ks[4], (DEPTH, DIFF_V_WIDTH, D_MODEL), DIFF_V_WIDTH ** -0.5),
        "w_out": n(ks[5], (DEPTH, D_MODEL, D_MODEL), D_MODEL ** -0.5),
        "lambda_q1": n(ks[6], (DEPTH, HEAD_DIM), 0.1),
        "lambda_k1": n(ks[7], (DEPTH, HEAD_DIM), 0.1),
        "lambda_q2": n(ks[8], (DEPTH, HEAD_DIM), 0.1),
        "lambda_k2": n(ks[9], (DEPTH, HEAD_DIM), 0.1),
        "diff_subln_g": gain(ks[10], DIFF_V_DIM),
        "g_mix_pre": gain(ks[11], D_MODEL),
        "g_mix_post": gain(ks[12], D_MODEL),
        "w_up": n(ks[13], (DEPTH, D_MODEL, D_FF), D_MODEL ** -0.5),
        "w_down": n(ks[14], (DEPTH, D_FF, D_MODEL), D_FF ** -0.5),
        "g_mlp_pre": gain(ks[15], D_MODEL),
        "g_mlp_post": gain(ks[16], D_MODEL),
        "w_ple_proj": n(ks[17], (DEPTH, PLE_DIM, D_MODEL), PLE_DIM ** -0.5),
        "w_ple_gate": n(ks[18], (DEPTH, D_MODEL, D_MODEL), D_MODEL ** -0.5),
        "g_ple_pre": gain(ks[19], D_MODEL),
        "g_ple_post": gain(ks[20], D_MODEL),
    }


def reference(x, p, w_in, w_br_moba, w_br_diff, w_out, lambda_q1, lambda_k1, lambda_q2, lambda_k2,
              diff_subln_g, g_mix_pre, g_mix_post, w_up, w_down, g_mlp_pre, g_mlp_post,
              w_ple_proj, w_ple_gate, g_ple_pre, g_ple_post):
    B, S, _ = x.shape
    pos_f = jnp.arange(S, dtype=F32)
    widths = [MOBA_WIDTH, MOBA_WIDTH, MOBA_WIDTH, DIFF_QK_WIDTH, DIFF_QK_WIDTH, DIFF_V_WIDTH, D_MODEL]
    split_points = [int(v) for v in np.cumsum(widths)]
    h = x
    for i in range(DEPTH):
        lambda_init = 0.8 - 0.6 * math.exp(-0.3 * i)
        u = rmsnorm(h, g_mix_pre[i])
        proj = u @ w_in[i]
        qa, ka, va, qb, kb, vb, ga, gb = jnp.split(proj, split_points, axis=-1)
        to_heads = lambda t: t.reshape(B, S, MOBA_HEADS, HEAD_DIM).transpose(0, 2, 1, 3)
        qa = partial_rope(to_heads(qa), pos_f)
        ka = partial_rope(to_heads(ka), pos_f)
        va = to_heads(va)
        oa = moba_attention(qa, ka, va)
        to_sub = lambda t: t.reshape(B, S, DIFF_HEADS, 2, HEAD_DIM).transpose(0, 2, 3, 1, 4)
        qb = partial_rope(to_sub(qb), pos_f)
        kb = partial_rope(to_sub(kb), pos_f)
        vb = vb.reshape(B, S, DIFF_HEADS, DIFF_V_DIM).transpose(0, 2, 1, 3)
        lam = (jnp.exp(jnp.sum(lambda_q1[i].astype(F32) * lambda_k1[i].astype(F32)))
               - jnp.exp(jnp.sum(lambda_q2[i].astype(F32) * lambda_k2[i].astype(F32)))
               + lambda_init)
        ob = diff_attention(qb, kb, vb, lam, diff_subln_g[i], lambda_init)
        ya = oa.transpose(0, 2, 1, 3).reshape(B, S, MOBA_WIDTH) @ w_br_moba[i]
        yb = ob.transpose(0, 2, 1, 3).reshape(B, S, DIFF_V_WIDTH) @ w_br_diff[i]
        mixed = jax.nn.sigmoid(ga) * ya + jax.nn.sigmoid(gb) * yb
        h = h + rmsnorm(mixed @ w_out[i], g_mix_post[i])
        u2 = rmsnorm(h, g_mlp_pre[i])
        ff = jnp.square(jax.nn.relu(u2 @ w_up[i])) @ w_down[i]
        h = h + rmsnorm(ff, g_mlp_post[i])
        gate = jax.nn.sigmoid(rmsnorm(h, g_ple_pre[i]) @ w_ple_gate[i])
        e = (p[i] @ w_ple_proj[i]) * gate
        h = h + rmsnorm(e, g_ple_post[i])
    return h
```

```python
import functools
import math

import jax
import jax.numpy as jnp
from jax import lax
from jax.experimental import pallas as pl
from jax.experimental.pallas import tpu as pltpu

F32 = jnp.float32
BF16 = jnp.bfloat16

D_MODEL = 2048
HEAD_DIM = 128
N_HEADS = 8
MOBA_BLOCK = 256
MOBA_TOPK = 3
DIFF_HEADS = 4
DIFF_V_DIM = 2 * HEAD_DIM
ROT_DIM = HEAD_DIM // 4
ROPE_THETA = 500000.0
D_FF = 4 * D_MODEL
PLE_DIM = 256
NORM_EPS = 1e-6
SECTION = N_HEADS * HEAD_DIM
QKV_WIDTH = 6 * SECTION
GATE_WIDTH = 2 * D_MODEL

VMEM_LIMIT_BYTES = 56 * 1024 * 1024

NEG_INF = float("-inf")
NT_DIMS = (((1,), (1,)), ((), ()))


def _params(n_axes):
    return pltpu.CompilerParams(dimension_semantics=("arbitrary",) * n_axes,
                                vmem_limit_bytes=VMEM_LIMIT_BYTES)


def _rms_scale(xf):
    return lax.rsqrt(jnp.mean(xf * xf, axis=-1, keepdims=True) + NORM_EPS)


def _qkv_kernel(x_ref, g_ref, w_ref, cos_ref, sin_lo_ref, sin_hi_ref,
                qa_ref, ka_ref, va_ref, qb_ref, kb_ref, vb_ref, kmean_ref, u_ref, *, tm):
    j = pl.program_id(1)

    @pl.when(j == 0)
    def _():
        xf = x_ref[...]
        u_ref[...] = (xf * _rms_scale(xf) * g_ref[...]).astype(BF16)

    y = jnp.dot(u_ref[...], w_ref[...], preferred_element_type=F32)

    def rope(yh):
        from_hi = pltpu.roll(yh, HEAD_DIM - ROT_DIM // 2, axis=1)
        from_lo = pltpu.roll(yh, ROT_DIM // 2, axis=1)
        return yh * cos_ref[...] + from_hi * sin_lo_ref[...] + from_lo * sin_hi_ref[...]

    def head(h):
        return y[:, h * HEAD_DIM:(h + 1) * HEAD_DIM]

    scale = HEAD_DIM ** -0.5

    @pl.when(j == 0)
    def _():
        for h in range(N_HEADS):
            qa_ref[h] = (rope(head(h)) * scale).astype(BF16)

    @pl.when(j == 1)
    def _():
        for h in range(N_HEADS):
            kr = rope(head(h))
            ka_ref[h] = kr.astype(BF16)
            kmean_ref[h] = jnp.mean(kr.reshape(tm // MOBA_BLOCK, MOBA_BLOCK, HEAD_DIM), axis=1)

    @pl.when(j == 2)
    def _():
        for h in range(N_HEADS):
            va_ref[h] = head(h).astype(BF16)

    @pl.when(j == 3)
    def _():
        for h in range(N_HEADS):
            qb_ref[h] = (rope(head(h)) * scale).astype(BF16)

    @pl.when(j == 4)
    def _():
        for h in range(N_HEADS):
            kb_ref[h] = rope(head(h)).astype(BF16)

    @pl.when(j == 5)
    def _():
        for h in range(DIFF_HEADS):
            vb_ref[h] = y[:, h * DIFF_V_DIM:(h + 1) * DIFF_V_DIM].astype(BF16)


def _qkv_proj(x2, g, w_qkv, cos_t, sin_lo, sin_hi, seq, tm=512):
    t = x2.shape[0]
    n_rows = t // tm
    pos_blocks = seq // tm
    head_out = jax.ShapeDtypeStruct((N_HEADS, t, HEAD_DIM), BF16)
    head_spec = pl.BlockSpec((N_HEADS, tm, HEAD_DIM), lambda i, j: (0, i, 0))
    tab_spec = pl.BlockSpec((tm, HEAD_DIM), lambda i, j: (i % pos_blocks, 0))
    return pl.pallas_call(
        functools.partial(_qkv_kernel, tm=tm),
        grid=(n_rows, QKV_WIDTH // SECTION),
        in_specs=[
            pl.BlockSpec((tm, D_MODEL), lambda i, j: (i, 0)),
            pl.BlockSpec((1, D_MODEL), lambda i, j: (0, 0)),
            pl.BlockSpec((D_MODEL, SECTION), lambda i, j: (0, j)),
            tab_spec, tab_spec, tab_spec,
        ],
        out_specs=[
            head_spec, head_spec, head_spec, head_spec, head_spec,
            pl.BlockSpec((DIFF_HEADS, tm, DIFF_V_DIM), lambda i, j: (0, i, 0)),
            pl.BlockSpec((None, N_HEADS, tm // MOBA_BLOCK, HEAD_DIM), lambda i, j: (i, 0, 0, 0)),
        ],
        out_shape=[
            head_out, head_out, head_out, head_out, head_out,
            jax.ShapeDtypeStruct((DIFF_HEADS, t, DIFF_V_DIM), BF16),
            jax.ShapeDtypeStruct((n_rows, N_HEADS, tm // MOBA_BLOCK, HEAD_DIM), F32),
        ],
        scratch_shapes=[pltpu.VMEM((tm, D_MODEL), BF16)],
        compiler_params=_params(2),
        name="qkv_proj",
    )(x2, g, w_qkv, cos_t, sin_lo, sin_hi)


def _gate_kernel(x_ref, g_ref, w_ref, o_ref, u_ref):
    @pl.when(pl.program_id(1) == 0)
    def _():
        xf = x_ref[...]
        u_ref[...] = (xf * _rms_scale(xf) * g_ref[...]).astype(BF16)

    o_ref[...] = jax.nn.sigmoid(jnp.dot(u_ref[...], w_ref[...], preferred_element_type=F32))


def _gate_proj(x2, g, w_gate, tm=512, tn=1024):
    t = x2.shape[0]
    return pl.pallas_call(
        _gate_kernel,
        grid=(t // tm, GATE_WIDTH // tn),
        in_specs=[
            pl.BlockSpec((tm, D_MODEL), lambda i, j: (i, 0)),
            pl.BlockSpec((1, D_MODEL), lambda i, j: (0, 0)),
            pl.BlockSpec((D_MODEL, tn), lambda i, j: (0, j)),
        ],
        out_specs=pl.BlockSpec((tm, tn), lambda i, j: (i, j)),
        out_shape=jax.ShapeDtypeStruct((t, GATE_WIDTH), F32),
        scratch_shapes=[pltpu.VMEM((tm, D_MODEL), BF16)],
        compiler_params=_params(2),
        name="gate_proj",
    )(x2, g, w_gate)


def _online_softmax_step(s, v, m_ref, l_ref, acc_ref):
    m_prev = m_ref[...]
    m_new = jnp.maximum(m_prev, jnp.max(s, axis=1, keepdims=True))
    alpha = jnp.exp(m_prev - m_new)
    p = jnp.exp(s - m_new)
    l_ref[...] = alpha * l_ref[...] + jnp.sum(p, axis=1, keepdims=True)
    acc_ref[...] = alpha * acc_ref[...] + jnp.dot(p.astype(BF16), v, preferred_element_type=F32)
    m_ref[...] = m_new


def _softmax_first_step(s, v, m_ref, l_ref, acc_ref):
    m = jnp.max(s, axis=1, keepdims=True)
    p = jnp.exp(s - m)
    m_ref[...] = m
    l_ref[...] = jnp.sum(p, axis=1, keepdims=True)
    acc_ref[...] = jnp.dot(p.astype(BF16), v, preferred_element_type=F32)


def _causal_mask(n):
    row = lax.broadcasted_iota(jnp.int32, (n, n), 0)
    col = lax.broadcasted_iota(jnp.int32, (n, n), 1)
    return col <= row


def _moba_kernel(q_ref, k_ref, v_ref, kmean_ref, o_ref, m_ref, l_ref, acc_ref, *, n_blocks):
    i = pl.program_id(2)
    q = q_ref[...]

    gate = lax.dot_general(q.astype(F32), kmean_ref[...], NT_DIMS,
                           precision=lax.Precision.HIGHEST, preferred_element_type=F32)
    blk = lax.broadcasted_iota(jnp.int32, gate.shape, 1).astype(F32)
    g = jnp.where(blk < i.astype(F32), gate, NEG_INF)
    picks = []
    for _ in range(MOBA_TOPK):
        best = jnp.max(g, axis=1, keepdims=True)
        first = jnp.min(jnp.where(g == best, blk, float(n_blocks)), axis=1, keepdims=True)
        first = jnp.where(best > NEG_INF, first, -1.0)
        picks.append(first)
        g = jnp.where(blk == first, NEG_INF, g)

    own = pl.multiple_of(i * MOBA_BLOCK, MOBA_BLOCK)
    s = lax.dot_general(q, k_ref[pl.ds(own, MOBA_BLOCK), :], NT_DIMS, preferred_element_type=F32)
    s = jnp.where(_causal_mask(MOBA_BLOCK), s, NEG_INF)
    _softmax_first_step(s, v_ref[pl.ds(own, MOBA_BLOCK), :], m_ref, l_ref, acc_ref)

    def body(j, carry):
        start = pl.multiple_of(j * MOBA_BLOCK, MOBA_BLOCK)
        jf = j.astype(F32)
        chosen = (picks[0] == jf) | (picks[1] == jf) | (picks[2] == jf)
        sj = lax.dot_general(q, k_ref[pl.ds(start, MOBA_BLOCK), :], NT_DIMS, preferred_element_type=F32)
        sj = jnp.where(chosen, sj, NEG_INF)
        _online_softmax_step(sj, v_ref[pl.ds(start, MOBA_BLOCK), :], m_ref, l_ref, acc_ref)
        return carry

    lax.fori_loop(0, i, body, 0)
    o_ref[...] = (acc_ref[...] / l_ref[...]).astype(o_ref.dtype)


def _moba_attention(qa, ka, va, kmean, batch, seq):
    t = batch * seq
    n_blocks = seq // MOBA_BLOCK
    return pl.pallas_call(
        functools.partial(_moba_kernel, n_blocks=n_blocks),
        grid=(batch, N_HEADS, n_blocks),
        in_specs=[
            pl.BlockSpec((None, MOBA_BLOCK, HEAD_DIM), lambda b, h, i: (h, b * n_blocks + i, 0)),
            pl.BlockSpec((None, seq, HEAD_DIM), lambda b, h, i: (h, b, 0)),
            pl.BlockSpec((None, seq, HEAD_DIM), lambda b, h, i: (h, b, 0)),
            pl.BlockSpec((None, n_blocks, HEAD_DIM), lambda b, h, i: (h, b, 0)),
        ],
        out_specs=pl.BlockSpec((MOBA_BLOCK, HEAD_DIM), lambda b, h, i: (b * n_blocks + i, h)),
        out_shape=jax.ShapeDtypeStruct((t, SECTION), BF16),
        scratch_shapes=[pltpu.VMEM((MOBA_BLOCK, 1), F32), pltpu.VMEM((MOBA_BLOCK, 1), F32),
                        pltpu.VMEM((MOBA_BLOCK, HEAD_DIM), F32)],
        compiler_params=_params(3),
        name="moba_attn",
    )(qa, ka, va, kmean)


def _diff_kernel(q0_ref, q1_ref, k0_ref, k1_ref, v_ref, lq1_ref, lk1_ref, lq2_ref, lk2_ref, subg_ref, o_ref,
                 m0_ref, l0_ref, acc0_ref, m1_ref, l1_ref, acc1_ref, *, tq, lambda_init):
    i = pl.program_id(2)
    q0 = q0_ref[...]
    q1 = q1_ref[...]
    own = pl.multiple_of(i * tq, tq)
    mask = _causal_mask(tq)
    v_own = v_ref[pl.ds(own, tq), :]
    for q, k_ref, m_ref, l_ref, acc_ref in ((q0, k0_ref, m0_ref, l0_ref, acc0_ref),
                                            (q1, k1_ref, m1_ref, l1_ref, acc1_ref)):
        s = lax.dot_general(q, k_ref[pl.ds(own, tq), :], NT_DIMS, preferred_element_type=F32)
        _softmax_first_step(jnp.where(mask, s, NEG_INF), v_own, m_ref, l_ref, acc_ref)

    def body(j, carry):
        start = pl.multiple_of(j * tq, tq)
        vj = v_ref[pl.ds(start, tq), :]
        for q, k_ref, m_ref, l_ref, acc_ref in ((q0, k0_ref, m0_ref, l0_ref, acc0_ref),
                                                (q1, k1_ref, m1_ref, l1_ref, acc1_ref)):
            sj = lax.dot_general(q, k_ref[pl.ds(start, tq), :], NT_DIMS, preferred_element_type=F32)
            _online_softmax_step(sj, vj, m_ref, l_ref, acc_ref)
        return carry

    lax.fori_loop(0, i, body, 0)

    lam = (jnp.exp(jnp.sum(lq1_ref[...] * lk1_ref[...], axis=1, keepdims=True))
           - jnp.exp(jnp.sum(lq2_ref[...] * lk2_ref[...], axis=1, keepdims=True)) + lambda_init)
    out = acc0_ref[...] / l0_ref[...] - lam * (acc1_ref[...] / l1_ref[...])
    out = out * _rms_scale(out) * subg_ref[...]
    o_ref[...] = (out * (1.0 - lambda_init)).astype(o_ref.dtype)


def _diff_attention(qb, kb, vb, lq1, lk1, lq2, lk2, sub_g, batch, seq, lambda_init, tq=256):
    t = batch * seq
    nq = seq // tq
    q_spec = lambda c: pl.BlockSpec((None, tq, HEAD_DIM), lambda b, h, i: (2 * h + c, b * nq + i, 0))
    k_spec = lambda c: pl.BlockSpec((None, seq, HEAD_DIM), lambda b, h, i: (2 * h + c, b, 0))
    vec_spec = pl.BlockSpec((1, HEAD_DIM), lambda b, h, i: (0, 0))
    stat = pltpu.VMEM((tq, 1), F32)
    acc = pltpu.VMEM((tq, DIFF_V_DIM), F32)
    return pl.pallas_call(
        functools.partial(_diff_kernel, tq=tq, lambda_init=lambda_init),
        grid=(batch, DIFF_HEADS, nq),
        in_specs=[
            q_spec(0), q_spec(1), k_spec(0), k_spec(1),
            pl.BlockSpec((None, seq, DIFF_V_DIM), lambda b, h, i: (h, b, 0)),
            vec_spec, vec_spec, vec_spec, vec_spec,
            pl.BlockSpec((1, DIFF_V_DIM), lambda b, h, i: (0, 0)),
        ],
        out_specs=pl.BlockSpec((tq, DIFF_V_DIM), lambda b, h, i: (b * nq + i, h)),
        out_shape=jax.ShapeDtypeStruct((t, DIFF_HEADS * DIFF_V_DIM), BF16),
        scratch_shapes=[stat, stat, acc, stat, stat, acc],
        compiler_params=_params(3),
        name="diff_attn",
    )(qb, qb, kb, kb, vb, lq1, lk1, lq2, lk2, sub_g)


def _mix_kernel(oa_ref, ob_ref, sga_ref, sgb_ref, x_ref, wm_ref, wd_ref, wo_ref, g_ref, o_ref):
    ya = jnp.dot(oa_ref[...], wm_ref[...], preferred_element_type=F32)
    yb = jnp.dot(ob_ref[...], wd_ref[...], preferred_element_type=F32)
    mixed = sga_ref[...] * ya + sgb_ref[...] * yb
    z = jnp.dot(mixed.astype(BF16), wo_ref[...], preferred_element_type=F32)
    o_ref[...] = x_ref[...] + z * _rms_scale(z) * g_ref[...]


def _mix(oa, ob, gates, x2, wm, wd, wo, g, tm=256):
    t = x2.shape[0]
    const = lambda shape: pl.BlockSpec(shape, lambda i: (0, 0))
    return pl.pallas_call(
        _mix_kernel,
        grid=(t // tm,),
        in_specs=[
            pl.BlockSpec((tm, SECTION), lambda i: (i, 0)),
            pl.BlockSpec((tm, SECTION), lambda i: (i, 0)),
            pl.BlockSpec((tm, D_MODEL), lambda i: (i, 0)),
            pl.BlockSpec((tm, D_MODEL), lambda i: (i, 1)),
            pl.BlockSpec((tm, D_MODEL), lambda i: (i, 0)),
            const((SECTION, D_MODEL)), const((SECTION, D_MODEL)), const((D_MODEL, D_MODEL)),
            const((1, D_MODEL)),
        ],
        out_specs=pl.BlockSpec((tm, D_MODEL), lambda i: (i, 0)),
        out_shape=jax.ShapeDtypeStruct((t, D_MODEL), F32),
        compiler_params=_params(1),
        name="mix_out",
    )(oa, ob, gates, gates, x2, wm, wd, wo, g)


def _mlp_kernel(h_ref, gpre_ref, wup_ref, wdown_ref, gpost_ref, o_ref, u_ref, acc_ref):
    k = pl.program_id(1)

    @pl.when(k == 0)
    def _():
        hf = h_ref[...]
        u_ref[...] = (hf * _rms_scale(hf) * gpre_ref[...]).astype(BF16)

    a = jnp.dot(u_ref[...], wup_ref[...], preferred_element_type=F32)
    a = jnp.square(jnp.maximum(a, 0.0)).astype(BF16)
    part = jnp.dot(a, wdown_ref[...], preferred_element_type=F32)

    @pl.when(k == 0)
    def _():
        acc_ref[...] = part

    @pl.when(k > 0)
    def _():
        acc_ref[...] += part

    @pl.when(k == pl.num_programs(1) - 1)
    def _():
        ff = acc_ref[...]
        o_ref[...] = h_ref[...] + ff * _rms_scale(ff) * gpost_ref[...]


def _mlp(h, gpre, wup, wdown, gpost, tm=512, tf=1024):
    t = h.shape[0]
    return pl.pallas_call(
        _mlp_kernel,
        grid=(t // tm, D_FF // tf),
        in_specs=[
            pl.BlockSpec((tm, D_MODEL), lambda i, k: (i, 0)),
            pl.BlockSpec((1, D_MODEL), lambda i, k: (0, 0)),
            pl.BlockSpec((D_MODEL, tf), lambda i, k: (0, k)),
            pl.BlockSpec((tf, D_MODEL), lambda i, k: (k, 0)),
            pl.BlockSpec((1, D_MODEL), lambda i, k: (0, 0)),
        ],
        out_specs=pl.BlockSpec((tm, D_MODEL), lambda i, k: (i, 0)),
        out_shape=jax.ShapeDtypeStruct((t, D_MODEL), F32),
        scratch_shapes=[pltpu.VMEM((tm, D_MODEL), BF16), pltpu.VMEM((tm, D_MODEL), F32)],
        compiler_params=_params(2),
        name="mlp",
    )(h, gpre, wup, wdown, gpost)


def _ple_kernel(h_ref, p_ref, gpre_ref, wgate_ref, wproj_ref, gpost_ref, o_ref):
    hf = h_ref[...]
    u = (hf * _rms_scale(hf) * gpre_ref[...]).astype(BF16)
    gate = jax.nn.sigmoid(jnp.dot(u, wgate_ref[...], preferred_element_type=F32))
    e = jnp.dot(p_ref[...].astype(BF16), wproj_ref[...], preferred_element_type=F32) * gate
    o_ref[...] = hf + e * _rms_scale(e) * gpost_ref[...]


def _ple(h, p2, gpre, wgate, wproj, gpost, tm=512):
    t = h.shape[0]
    const = lambda shape: pl.BlockSpec(shape, lambda i: (0, 0))
    return pl.pallas_call(
        _ple_kernel,
        grid=(t // tm,),
        in_specs=[
            pl.BlockSpec((tm, D_MODEL), lambda i: (i, 0)),
            pl.BlockSpec((tm, PLE_DIM), lambda i: (i, 0)),
            const((1, D_MODEL)), const((D_MODEL, D_MODEL)), const((PLE_DIM, D_MODEL)), const((1, D_MODEL)),
        ],
        out_specs=pl.BlockSpec((tm, D_MODEL), lambda i: (i, 0)),
        out_shape=jax.ShapeDtypeStruct((t, D_MODEL), F32),
        compiler_params=_params(1),
        name="ple",
    )(h, p2, gpre, wgate, wproj, gpost)


def _rope_tables(seq):
    half = ROT_DIM // 2
    pos_f = jnp.arange(seq, dtype=F32)
    inv_freq = 1.0 / (ROPE_THETA ** (jnp.arange(half, dtype=F32) * 2.0 / ROT_DIM))
    ang = pos_f[:, None] * inv_freq[None, :]
    cos, sin = jnp.cos(ang), jnp.sin(ang)
    pad = HEAD_DIM - ROT_DIM
    zeros = jnp.zeros((seq, half), F32)
    cos_t = jnp.concatenate([cos, cos, jnp.ones((seq, pad), F32)], axis=1)
    sin_lo = jnp.concatenate([-sin, zeros, jnp.zeros((seq, pad), F32)], axis=1)
    sin_hi = jnp.concatenate([zeros, sin, jnp.zeros((seq, pad), F32)], axis=1)
    return cos_t, sin_lo, sin_hi


def kernel(x, p, w_in, w_br_moba, w_br_diff, w_out, lambda_q1, lambda_k1, lambda_q2, lambda_k2, diff_subln_g,
           g_mix_pre, g_mix_post, w_up, w_down, g_mlp_pre, g_mlp_post, w_ple_proj, w_ple_gate, g_ple_pre,
           g_ple_post):
    batch, seq, _ = x.shape
    depth = w_in.shape[0]
    t = batch * seq
    n_blocks = seq // MOBA_BLOCK
    cos_t, sin_lo, sin_hi = _rope_tables(seq)
    row = lambda v: v.reshape(1, -1).astype(F32)

    h = x.reshape(t, D_MODEL)
    for layer in range(depth):
        lambda_init = 0.8 - 0.6 * math.exp(-0.3 * layer)
        w_l = w_in[layer].astype(BF16)
        qa, ka, va, qb, kb, vb, kmean = _qkv_proj(h, row(g_mix_pre[layer]), w_l[:, :QKV_WIDTH],
                                                  cos_t, sin_lo, sin_hi, seq)
        gates = _gate_proj(h, row(g_mix_pre[layer]), w_l[:, QKV_WIDTH:])
        kmean = kmean.transpose(1, 0, 2, 3).reshape(N_HEADS, batch * n_blocks, HEAD_DIM)
        oa = _moba_attention(qa, ka, va, kmean, batch, seq)
        ob = _diff_attention(qb, kb, vb, row(lambda_q1[layer]), row(lambda_k1[layer]), row(lambda_q2[layer]),
                             row(lambda_k2[layer]), row(diff_subln_g[layer]), batch, seq, lambda_init)
        h = _mix(oa, ob, gates, h, w_br_moba[layer].astype(BF16), w_br_diff[layer].astype(BF16),
                 w_out[layer].astype(BF16), row(g_mix_post[layer]))
        h = _mlp(h, row(g_mlp_pre[layer]), w_up[layer].astype(BF16), w_down[layer].astype(BF16),
                 row(g_mlp_post[layer]))
        h = _ple(h, p[layer].reshape(t, PLE_DIM), row(g_ple_pre[layer]), w_ple_gate[layer].astype(BF16),
                 w_ple_proj[layer].astype(BF16), row(g_ple_post[layer]))
    return h.reshape(batch, seq, D_MODEL)
```

```python
import functools
import math

import jax
import jax.numpy as jnp
from jax import lax
from jax.experimental import pallas as pl
from jax.experimental.pallas import tpu as pltpu

F32 = jnp.float32
BF16 = jnp.bfloat16

D_MODEL = 2048
HEAD_DIM = 128
N_HEADS = 8
MOBA_BLOCK = 256
MOBA_TOPK = 3
DIFF_HEADS = 4
DIFF_V_DIM = 2 * HEAD_DIM
ROT_DIM = HEAD_DIM // 4
ROPE_THETA = 500000.0
D_FF = 4 * D_MODEL
PLE_DIM = 256
NORM_EPS = 1e-6
SECTION = N_HEADS * HEAD_DIM
QKV_WIDTH = 6 * SECTION
GATE_WIDTH = 2 * D_MODEL

ATTN_TQ = 256
ATTN_CHUNK = 1024
BLOCKS_PER_CHUNK = ATTN_CHUNK // MOBA_BLOCK
TILES_PER_STEP = ATTN_CHUNK // ATTN_TQ

VMEM_LIMIT_BYTES = 56 * 1024 * 1024

NEG_INF = float("-inf")
NT_DIMS = (((1,), (1,)), ((), ()))
Q_SCALE = HEAD_DIM ** -0.5 * math.log2(math.e)


def _params(n_axes):
    return pltpu.CompilerParams(dimension_semantics=("arbitrary",) * n_axes,
                                vmem_limit_bytes=VMEM_LIMIT_BYTES)


def _rms_scale(xf):
    return lax.rsqrt(jnp.mean(xf * xf, axis=-1, keepdims=True) + NORM_EPS)


def _qkv_kernel(x_ref, g_ref, w_ref, cos_ref, sin_lo_ref, sin_hi_ref,
                qa_ref, ka_ref, va_ref, qb_ref, kb_ref, vb_ref, kmean_ref, u_ref, *, tm):
    j = pl.program_id(1)

    @pl.when(j == 0)
    def _():
        xf = x_ref[...]
        u_ref[...] = (xf * _rms_scale(xf) * g_ref[...]).astype(BF16)

    y = jnp.dot(u_ref[...], w_ref[...], preferred_element_type=F32)

    def rope(yh):
        from_hi = pltpu.roll(yh, HEAD_DIM - ROT_DIM // 2, axis=1)
        from_lo = pltpu.roll(yh, ROT_DIM // 2, axis=1)
        return yh * cos_ref[...] + from_hi * sin_lo_ref[...] + from_lo * sin_hi_ref[...]

    def head(h):
        return y[:, h * HEAD_DIM:(h + 1) * HEAD_DIM]

    @pl.when(j == 0)
    def _():
        for h in range(N_HEADS):
            qa_ref[h] = (rope(head(h)) * Q_SCALE).T.astype(BF16)

    @pl.when(j == 1)
    def _():
        for h in range(N_HEADS):
            kr = rope(head(h))
            ka_ref[h] = kr.astype(BF16)
            kmean_ref[h] = jnp.mean(kr.reshape(tm // MOBA_BLOCK, MOBA_BLOCK, HEAD_DIM), axis=1)

    @pl.when(j == 2)
    def _():
        for h in range(N_HEADS):
            va_ref[h] = head(h).T.astype(BF16)

    @pl.when(j == 3)
    def _():
        for h in range(N_HEADS):
            qb_ref[h] = (rope(head(h)) * Q_SCALE).T.astype(BF16)

    @pl.when(j == 4)
    def _():
        for h in range(N_HEADS):
            kb_ref[h] = rope(head(h)).astype(BF16)

    @pl.when(j == 5)
    def _():
        for h in range(N_HEADS):
            vb_ref[h] = head(h).T.astype(BF16)


def _qkv_proj(x2, g, w_qkv, cos_t, sin_lo, sin_hi, seq, tm=512):
    t = x2.shape[0]
    n_rows = t // tm
    pos_blocks = seq // tm
    row_major = jax.ShapeDtypeStruct((N_HEADS, t, HEAD_DIM), BF16)
    col_major = jax.ShapeDtypeStruct((N_HEADS, HEAD_DIM, t), BF16)
    row_spec = pl.BlockSpec((N_HEADS, tm, HEAD_DIM), lambda i, j: (0, i, 0))
    col_spec = pl.BlockSpec((N_HEADS, HEAD_DIM, tm), lambda i, j: (0, 0, i))
    tab_spec = pl.BlockSpec((tm, HEAD_DIM), lambda i, j: (i % pos_blocks, 0))
    return pl.pallas_call(
        functools.partial(_qkv_kernel, tm=tm),
        grid=(n_rows, QKV_WIDTH // SECTION),
        in_specs=[
            pl.BlockSpec((tm, D_MODEL), lambda i, j: (i, 0)),
            pl.BlockSpec((1, D_MODEL), lambda i, j: (0, 0)),
            pl.BlockSpec((D_MODEL, SECTION), lambda i, j: (0, j)),
            tab_spec, tab_spec, tab_spec,
        ],
        out_specs=[
            col_spec, row_spec, col_spec, col_spec, row_spec, col_spec,
            pl.BlockSpec((None, N_HEADS, tm // MOBA_BLOCK, HEAD_DIM), lambda i, j: (i, 0, 0, 0)),
        ],
        out_shape=[
            col_major, row_major, col_major, col_major, row_major, col_major,
            jax.ShapeDtypeStruct((n_rows, N_HEADS, tm // MOBA_BLOCK, HEAD_DIM), F32),
        ],
        scratch_shapes=[pltpu.VMEM((tm, D_MODEL), BF16)],
        compiler_params=_params(2),
        name="qkv_proj",
    )(x2, g, w_qkv, cos_t, sin_lo, sin_hi)


def _gate_kernel(x_ref, g_ref, w_ref, o_ref, u_ref):
    @pl.when(pl.program_id(1) == 0)
    def _():
        xf = x_ref[...]
        u_ref[...] = (xf * _rms_scale(xf) * g_ref[...]).astype(BF16)

    o_ref[...] = jax.nn.sigmoid(jnp.dot(u_ref[...], w_ref[...], preferred_element_type=F32))


def _gate_proj(x2, g, w_gate, tm=512, tn=1024):
    t = x2.shape[0]
    return pl.pallas_call(
        _gate_kernel,
        grid=(t // tm, GATE_WIDTH // tn),
        in_specs=[
            pl.BlockSpec((tm, D_MODEL), lambda i, j: (i, 0)),
            pl.BlockSpec((1, D_MODEL), lambda i, j: (0, 0)),
            pl.BlockSpec((D_MODEL, tn), lambda i, j: (0, j)),
        ],
        out_specs=pl.BlockSpec((tm, tn), lambda i, j: (i, j)),
        out_shape=jax.ShapeDtypeStruct((t, GATE_WIDTH), F32),
        scratch_shapes=[pltpu.VMEM((tm, D_MODEL), BF16)],
        compiler_params=_params(2),
        name="gate_proj",
    )(x2, g, w_gate)


def _softmax_start(s_t):
    m = jnp.max(s_t, axis=0, keepdims=True)
    p_t = jnp.exp2(s_t - m)
    return m, jnp.sum(p_t, axis=0, keepdims=True), p_t.astype(BF16)


def _softmax_step(s_t, m, l):
    m_new = jnp.maximum(m, jnp.max(s_t, axis=0, keepdims=True))
    alpha = jnp.exp2(m - m_new)
    p_t = jnp.exp2(s_t - m_new)
    return m_new, alpha * l + jnp.sum(p_t, axis=0, keepdims=True), alpha, p_t.astype(BF16)


def _causal_bias():
    key = lax.broadcasted_iota(jnp.int32, (ATTN_TQ, ATTN_TQ), 0)
    qry = lax.broadcasted_iota(jnp.int32, (ATTN_TQ, ATTN_TQ), 1)
    return jnp.where(key <= qry, 0.0, NEG_INF)


def _moba_kernel(q_ref, k_ref, v_ref, kmean_ref, o_ref, acc_ref, *, n_blocks):
    sup = pl.program_id(2)
    q_all = q_ref[...]

    gate = jnp.dot(kmean_ref[...], q_all.astype(F32), precision=lax.Precision.HIGHEST,
                   preferred_element_type=F32)
    blk = lax.broadcasted_iota(jnp.int32, gate.shape, 0).astype(F32)
    own_blk = (sup * TILES_PER_STEP
               + lax.broadcasted_iota(jnp.int32, (1, ATTN_CHUNK), 1) // MOBA_BLOCK).astype(F32)
    g = jnp.where(blk < own_blk, gate, NEG_INF)
    picks = []
    for _ in range(MOBA_TOPK):
        best = jnp.max(g, axis=0, keepdims=True)
        first = jnp.min(jnp.where(g == best, blk, float(n_blocks)), axis=0, keepdims=True)
        first = jnp.where(best > NEG_INF, first, -1.0)
        picks.append(first)
        g = jnp.where(blk == first, NEG_INF, g)

    def lanes(a):
        return slice(a * ATTN_TQ, (a + 1) * ATTN_TQ)

    q_t = [q_all[:, lanes(a)] for a in range(TILES_PER_STEP)]
    tile_picks = [[p[:, lanes(a)] for p in picks] for a in range(TILES_PER_STEP)]

    def selection_bias(a, block_index):
        jf = block_index.astype(F32)
        chosen = (tile_picks[a][0] == jf) | (tile_picks[a][1] == jf) | (tile_picks[a][2] == jf)
        return jnp.where(chosen, 0.0, NEG_INF)

    def block_rows(s_t, r):
        return s_t[r * MOBA_BLOCK:(r + 1) * MOBA_BLOCK]

    own = pl.multiple_of(sup * ATTN_CHUNK, ATTN_CHUNK)
    causal_bias = _causal_bias()
    stats = []
    for a in range(TILES_PER_STEP):
        rows = (a + 1) * MOBA_BLOCK
        s_t = jnp.dot(k_ref[pl.ds(own, rows), :], q_t[a], preferred_element_type=F32)
        parts = [block_rows(s_t, r) + selection_bias(a, sup * TILES_PER_STEP + r) for r in range(a)]
        parts.append(block_rows(s_t, a) + causal_bias)
        m, l, p_t = _softmax_start(jnp.concatenate(parts, axis=0))
        acc_ref[a] = jnp.dot(v_ref[:, pl.ds(own, rows)], p_t, preferred_element_type=F32)
        stats.append((m, l))

    def body(c, stats):
        start = pl.multiple_of(c * ATTN_CHUNK, ATTN_CHUNK)
        k_c = k_ref[pl.ds(start, ATTN_CHUNK), :]
        v_c = v_ref[:, pl.ds(start, ATTN_CHUNK)]
        out = []
        for a in range(TILES_PER_STEP):
            s_t = jnp.dot(k_c, q_t[a], preferred_element_type=F32)
            s_t = jnp.concatenate([block_rows(s_t, r) + selection_bias(a, c * BLOCKS_PER_CHUNK + r)
                                   for r in range(BLOCKS_PER_CHUNK)], axis=0)
            m, l, alpha, p_t = _softmax_step(s_t, *stats[a])
            acc_ref[a] = alpha * acc_ref[a] + jnp.dot(v_c, p_t, preferred_element_type=F32)
            out.append((m, l))
        return tuple(out)

    stats = lax.fori_loop(0, sup, body, tuple(stats))
    for a in range(TILES_PER_STEP):
        o_ref[lanes(a), :] = (acc_ref[a] / stats[a][1]).T.astype(o_ref.dtype)


def _moba_attention(qa_t, ka, va_t, kmean, batch, seq):
    t = batch * seq
    n_blocks = seq // MOBA_BLOCK
    n_steps = seq // ATTN_CHUNK
    return pl.pallas_call(
        functools.partial(_moba_kernel, n_blocks=n_blocks),
        grid=(batch, N_HEADS, n_steps),
        in_specs=[
            pl.BlockSpec((None, HEAD_DIM, ATTN_CHUNK), lambda b, h, i: (h, 0, b * n_steps + i)),
            pl.BlockSpec((None, seq, HEAD_DIM), lambda b, h, i: (h, b, 0)),
            pl.BlockSpec((None, HEAD_DIM, seq), lambda b, h, i: (h, 0, b)),
            pl.BlockSpec((None, n_blocks, HEAD_DIM), lambda b, h, i: (h, b, 0)),
        ],
        out_specs=pl.BlockSpec((ATTN_CHUNK, HEAD_DIM), lambda b, h, i: (b * n_steps + i, h)),
        out_shape=jax.ShapeDtypeStruct((t, SECTION), BF16),
        scratch_shapes=[pltpu.VMEM((TILES_PER_STEP, HEAD_DIM, ATTN_TQ), F32)],
        compiler_params=_params(3),
        name="moba_attn",
    )(qa_t, ka, va_t, kmean)


def _diff_kernel(q0_ref, q1_ref, k0_ref, k1_ref, vlo_ref, vhi_ref, lq1_ref, lk1_ref, lq2_ref, lk2_ref, subg_ref,
                 o_ref, acc_ref, *, lambda_init):
    sup = pl.program_id(2)
    q_refs = (q0_ref, q1_ref)
    k_refs = (k0_ref, k1_ref)
    chains = [(a, sub) for a in range(TILES_PER_STEP) for sub in range(2)]
    q_t = {(a, sub): q_refs[sub][:, a * ATTN_TQ:(a + 1) * ATTN_TQ] for a, sub in chains}

    def values(start, rows):
        return jnp.concatenate([vlo_ref[:, pl.ds(start, rows)], vhi_ref[:, pl.ds(start, rows)]], axis=0)

    own = pl.multiple_of(sup * ATTN_CHUNK, ATTN_CHUNK)
    causal_bias = _causal_bias()
    stats = []
    for n, (a, sub) in enumerate(chains):
        rows = (a + 1) * ATTN_TQ
        s_t = jnp.dot(k_refs[sub][pl.ds(own, rows), :], q_t[a, sub], preferred_element_type=F32)
        parts = ([s_t[:a * ATTN_TQ]] if a else []) + [s_t[a * ATTN_TQ:] + causal_bias]
        m, l, p_t = _softmax_start(jnp.concatenate(parts, axis=0))
        acc_ref[n] = jnp.dot(values(own, rows), p_t, preferred_element_type=F32)
        stats.append((m, l))

    def body(c, stats):
        start = pl.multiple_of(c * ATTN_CHUNK, ATTN_CHUNK)
        k_c = [k_refs[sub][pl.ds(start, ATTN_CHUNK), :] for sub in range(2)]
        v_c = values(start, ATTN_CHUNK)
        out = []
        for n, (a, sub) in enumerate(chains):
            s_t = jnp.dot(k_c[sub], q_t[a, sub], preferred_element_type=F32)
            m, l, alpha, p_t = _softmax_step(s_t, *stats[n])
            acc_ref[n] = alpha * acc_ref[n] + jnp.dot(v_c, p_t, preferred_element_type=F32)
            out.append((m, l))
        return tuple(out)

    stats = lax.fori_loop(0, sup, body, tuple(stats))

    lam = (jnp.exp(jnp.sum(lq1_ref[...] * lk1_ref[...], axis=1, keepdims=True))
           - jnp.exp(jnp.sum(lq2_ref[...] * lk2_ref[...], axis=1, keepdims=True)) + lambda_init)
    for a in range(TILES_PER_STEP):
        out = acc_ref[2 * a] / stats[2 * a][1] - lam * (acc_ref[2 * a + 1] / stats[2 * a + 1][1])
        inv = lax.rsqrt(jnp.mean(out * out, axis=0, keepdims=True) + NORM_EPS)
        out = out * inv * subg_ref[...]
        o_ref[a * ATTN_TQ:(a + 1) * ATTN_TQ, :] = (out * (1.0 - lambda_init)).T.astype(o_ref.dtype)


def _diff_attention(qb_t, kb, vb_t, lq1, lk1, lq2, lk2, sub_g_col, batch, seq, lambda_init):
    t = batch * seq
    n_steps = seq // ATTN_CHUNK
    q_spec = lambda c: pl.BlockSpec((None, HEAD_DIM, ATTN_CHUNK), lambda b, h, i: (2 * h + c, 0, b * n_steps + i))
    k_spec = lambda c: pl.BlockSpec((None, seq, HEAD_DIM), lambda b, h, i: (2 * h + c, b, 0))
    v_spec = lambda c: pl.BlockSpec((None, HEAD_DIM, seq), lambda b, h, i: (2 * h + c, 0, b))
    vec_spec = pl.BlockSpec((1, HEAD_DIM), lambda b, h, i: (0, 0))
    return pl.pallas_call(
        functools.partial(_diff_kernel, lambda_init=lambda_init),
        grid=(batch, DIFF_HEADS, n_steps),
        in_specs=[
            q_spec(0), q_spec(1), k_spec(0), k_spec(1), v_spec(0), v_spec(1),
            vec_spec, vec_spec, vec_spec, vec_spec,
            pl.BlockSpec((DIFF_V_DIM, 1), lambda b, h, i: (0, 0)),
        ],
        out_specs=pl.BlockSpec((ATTN_CHUNK, DIFF_V_DIM), lambda b, h, i: (b * n_steps + i, h)),
        out_shape=jax.ShapeDtypeStruct((t, DIFF_HEADS * DIFF_V_DIM), BF16),
        scratch_shapes=[pltpu.VMEM((2 * TILES_PER_STEP, DIFF_V_DIM, ATTN_TQ), F32)],
        compiler_params=_params(3),
        name="diff_attn",
    )(qb_t, qb_t, kb, kb, vb_t, vb_t, lq1, lk1, lq2, lk2, sub_g_col)


def _mix_kernel(oa_ref, ob_ref, sga_ref, sgb_ref, x_ref, wm_ref, wd_ref, wo_ref, g_ref, o_ref):
    ya = jnp.dot(oa_ref[...], wm_ref[...], preferred_element_type=F32)
    yb = jnp.dot(ob_ref[...], wd_ref[...], preferred_element_type=F32)
    mixed = sga_ref[...] * ya + sgb_ref[...] * yb
    z = jnp.dot(mixed.astype(BF16), wo_ref[...], preferred_element_type=F32)
    o_ref[...] = x_ref[...] + z * _rms_scale(z) * g_ref[...]


def _mix(oa, ob, gates, x2, wm, wd, wo, g, tm=256):
    t = x2.shape[0]
    const = lambda shape: pl.BlockSpec(shape, lambda i: (0, 0))
    return pl.pallas_call(
        _mix_kernel,
        grid=(t // tm,),
        in_specs=[
            pl.BlockSpec((tm, SECTION), lambda i: (i, 0)),
            pl.BlockSpec((tm, SECTION), lambda i: (i, 0)),
            pl.BlockSpec((tm, D_MODEL), lambda i: (i, 0)),
            pl.BlockSpec((tm, D_MODEL), lambda i: (i, 1)),
            pl.BlockSpec((tm, D_MODEL), lambda i: (i, 0)),
            const((SECTION, D_MODEL)), const((SECTION, D_MODEL)), const((D_MODEL, D_MODEL)),
            const((1, D_MODEL)),
        ],
        out_specs=pl.BlockSpec((tm, D_MODEL), lambda i: (i, 0)),
        out_shape=jax.ShapeDtypeStruct((t, D_MODEL), F32),
        compiler_params=_params(1),
        name="mix_out",
    )(oa, ob, gates, gates, x2, wm, wd, wo, g)


def _mlp_kernel(h_ref, gpre_ref, wup_ref, wdown_ref, gpost_ref, o_ref, u_ref, acc_ref):
    k = pl.program_id(1)

    @pl.when(k == 0)
    def _():
        hf = h_ref[...]
        u_ref[...] = (hf * _rms_scale(hf) * gpre_ref[...]).astype(BF16)

    a = jnp.dot(u_ref[...], wup_ref[...], preferred_element_type=F32)
    a = jnp.square(jnp.maximum(a, 0.0)).astype(BF16)
    part = jnp.dot(a, wdown_ref[...], preferred_element_type=F32)

    @pl.when(k == 0)
    def _():
        acc_ref[...] = part

    @pl.when(k > 0)
    def _():
        acc_ref[...] += part

    @pl.when(k == pl.num_programs(1) - 1)
    def _():
        ff = acc_ref[...]
        o_ref[...] = h_ref[...] + ff * _rms_scale(ff) * gpost_ref[...]


def _mlp(h, gpre, wup, wdown, gpost, tm=512, tf=1024):
    t = h.shape[0]
    return pl.pallas_call(
        _mlp_kernel,
        grid=(t // tm, D_FF // tf),
        in_specs=[
            pl.BlockSpec((tm, D_MODEL), lambda i, k: (i, 0)),
            pl.BlockSpec((1, D_MODEL), lambda i, k: (0, 0)),
            pl.BlockSpec((D_MODEL, tf), lambda i, k: (0, k)),
            pl.BlockSpec((tf, D_MODEL), lambda i, k: (k, 0)),
            pl.BlockSpec((1, D_MODEL), lambda i, k: (0, 0)),
        ],
        out_specs=pl.BlockSpec((tm, D_MODEL), lambda i, k: (i, 0)),
        out_shape=jax.ShapeDtypeStruct((t, D_MODEL), F32),
        scratch_shapes=[pltpu.VMEM((tm, D_MODEL), BF16), pltpu.VMEM((tm, D_MODEL), F32)],
        compiler_params=_params(2),
        name="mlp",
    )(h, gpre, wup, wdown, gpost)


def _ple_kernel(h_ref, p_ref, gpre_ref, wgate_ref, wproj_ref, gpost_ref, o_ref):
    hf = h_ref[...]
    u = (hf * _rms_scale(hf) * gpre_ref[...]).astype(BF16)
    gate = jax.nn.sigmoid(jnp.dot(u, wgate_ref[...], preferred_element_type=F32))
    e = jnp.dot(p_ref[...].astype(BF16), wproj_ref[...], preferred_element_type=F32) * gate
    o_ref[...] = hf + e * _rms_scale(e) * gpost_ref[...]


def _ple(h, p2, gpre, wgate, wproj, gpost, tm=512):
    t = h.shape[0]
    const = lambda shape: pl.BlockSpec(shape, lambda i: (0, 0))
    return pl.pallas_call(
        _ple_kernel,
        grid=(t // tm,),
        in_specs=[
            pl.BlockSpec((tm, D_MODEL), lambda i: (i, 0)),
            pl.BlockSpec((tm, PLE_DIM), lambda i: (i, 0)),
            const((1, D_MODEL)), const((D_MODEL, D_MODEL)), const((PLE_DIM, D_MODEL)), const((1, D_MODEL)),
        ],
        out_specs=pl.BlockSpec((tm, D_MODEL), lambda i: (i, 0)),
        out_shape=jax.ShapeDtypeStruct((t, D_MODEL), F32),
        compiler_params=_params(1),
        name="ple",
    )(h, p2, gpre, wgate, wproj, gpost)


def _rope_tables(seq):
    half = ROT_DIM // 2
    pos_f = jnp.arange(seq, dtype=F32)
    inv_freq = 1.0 / (ROPE_THETA ** (jnp.arange(half, dtype=F32) * 2.0 / ROT_DIM))
    ang = pos_f[:, None] * inv_freq[None, :]
    cos, sin = jnp.cos(ang), jnp.sin(ang)
    pad = HEAD_DIM - ROT_DIM
    zeros = jnp.zeros((seq, half), F32)
    cos_t = jnp.concatenate([cos, cos, jnp.ones((seq, pad), F32)], axis=1)
    sin_lo = jnp.concatenate([-sin, zeros, jnp.zeros((seq, pad), F32)], axis=1)
    sin_hi = jnp.concatenate([zeros, sin, jnp.zeros((seq, pad), F32)], axis=1)
    return cos_t, sin_lo, sin_hi


def kernel(x, p, w_in, w_br_moba, w_br_diff, w_out, lambda_q1, lambda_k1, lambda_q2, lambda_k2, diff_subln_g,
           g_mix_pre, g_mix_post, w_up, w_down, g_mlp_pre, g_mlp_post, w_ple_proj, w_ple_gate, g_ple_pre,
           g_ple_post):
    batch, seq, _ = x.shape
    depth = w_in.shape[0]
    t = batch * seq
    n_blocks = seq // MOBA_BLOCK
    cos_t, sin_lo, sin_hi = _rope_tables(seq)
    row = lambda v: v.reshape(1, -1).astype(F32)

    h = x.reshape(t, D_MODEL)
    for layer in range(depth):
        lambda_init = 0.8 - 0.6 * math.exp(-0.3 * layer)
        w_l = w_in[layer].astype(BF16)
        qa_t, ka, va_t, qb_t, kb, vb_t, kmean = _qkv_proj(h, row(g_mix_pre[layer]), w_l[:, :QKV_WIDTH],
                                                          cos_t, sin_lo, sin_hi, seq)
        gates = _gate_proj(h, row(g_mix_pre[layer]), w_l[:, QKV_WIDTH:])
        kmean = kmean.transpose(1, 0, 2, 3).reshape(N_HEADS, batch * n_blocks, HEAD_DIM)
        oa = _moba_attention(qa_t, ka, va_t, kmean, batch, seq)
        ob = _diff_attention(qb_t, kb, vb_t, row(lambda_q1[layer]), row(lambda_k1[layer]), row(lambda_q2[layer]),
                             row(lambda_k2[layer]), diff_subln_g[layer].reshape(-1, 1).astype(F32),
                             batch, seq, lambda_init)
        h = _mix(oa, ob, gates, h, w_br_moba[layer].astype(BF16), w_br_diff[layer].astype(BF16),
                 w_out[layer].astype(BF16), row(g_mix_post[layer]))
        h = _mlp(h, row(g_mlp_pre[layer]), w_up[layer].astype(BF16), w_down[layer].astype(BF16),
                 row(g_mlp_post[layer]))
        h = _ple(h, p[layer].reshape(t, PLE_DIM), row(g_ple_pre[layer]), w_ple_gate[layer].astype(BF16),
                 w_ple_proj[layer].astype(BF16), row(g_ple_post[layer]))
    return h.reshape(batch, seq, D_MODEL)
```

```python
import functools
import math

import jax
import jax.numpy as jnp
from jax import lax
from jax.experimental import pallas as pl
from jax.experimental.pallas import tpu as pltpu

F32 = jnp.float32
BF16 = jnp.bfloat16

D_MODEL = 2048
HEAD_DIM = 128
N_HEADS = 8
MOBA_BLOCK = 256
MOBA_TOPK = 3
DIFF_HEADS = 4
DIFF_V_DIM = 2 * HEAD_DIM
ROT_DIM = HEAD_DIM // 4
ROPE_THETA = 500000.0
D_FF = 4 * D_MODEL
PLE_DIM = 256
NORM_EPS = 1e-6
SECTION = N_HEADS * HEAD_DIM
QKV_WIDTH = 6 * SECTION
GATE_WIDTH = 2 * D_MODEL

ATTN_TQ = 256
ATTN_CHUNK = 1024
BLOCKS_PER_CHUNK = ATTN_CHUNK // MOBA_BLOCK
TILES_PER_STEP = ATTN_CHUNK // ATTN_TQ
ONES_ROWS = 16

VMEM_LIMIT_BYTES = 56 * 1024 * 1024

NEG_INF = float("-inf")
NT_DIMS = (((1,), (1,)), ((), ()))
Q_SCALE = HEAD_DIM ** -0.5 * math.log2(math.e)


def _params(n_axes):
    return pltpu.CompilerParams(dimension_semantics=("arbitrary",) * n_axes,
                                vmem_limit_bytes=VMEM_LIMIT_BYTES)


def _rms_scale(xf):
    return lax.rsqrt(jnp.mean(xf * xf, axis=-1, keepdims=True) + NORM_EPS)


def _qkv_kernel(x_ref, g_ref, w_ref, cos_ref, sin_lo_ref, sin_hi_ref,
                qa_ref, ka_ref, va_ref, qb_ref, kb_ref, vb_ref, kmean_ref, u_ref, *, tm):
    j = pl.program_id(1)

    @pl.when(j == 0)
    def _():
        xf = x_ref[...]
        u_ref[...] = (xf * _rms_scale(xf) * g_ref[...]).astype(BF16)

    y = jnp.dot(u_ref[...], w_ref[...], preferred_element_type=F32)

    def rope(yh):
        from_hi = pltpu.roll(yh, HEAD_DIM - ROT_DIM // 2, axis=1)
        from_lo = pltpu.roll(yh, ROT_DIM // 2, axis=1)
        return yh * cos_ref[...] + from_hi * sin_lo_ref[...] + from_lo * sin_hi_ref[...]

    def head(h):
        return y[:, h * HEAD_DIM:(h + 1) * HEAD_DIM]

    @pl.when(j == 0)
    def _():
        for h in range(N_HEADS):
            qa_ref[h] = (rope(head(h)) * Q_SCALE).T.astype(BF16)

    @pl.when(j == 1)
    def _():
        for h in range(N_HEADS):
            kr = rope(head(h))
            ka_ref[h] = kr.astype(BF16)
            kmean_ref[h] = jnp.mean(kr.reshape(tm // MOBA_BLOCK, MOBA_BLOCK, HEAD_DIM), axis=1)

    @pl.when(j == 2)
    def _():
        ones_tile = jnp.where(lax.broadcasted_iota(jnp.int32, (ONES_ROWS, tm), 0) == 0, 1.0, 0.0).astype(BF16)
        for h in range(N_HEADS):
            va_ref[h, :HEAD_DIM] = head(h).T.astype(BF16)
            va_ref[h, HEAD_DIM:] = ones_tile

    @pl.when(j == 3)
    def _():
        for h in range(N_HEADS):
            qb_ref[h] = (rope(head(h)) * Q_SCALE).T.astype(BF16)

    @pl.when(j == 4)
    def _():
        for h in range(N_HEADS):
            kb_ref[h] = rope(head(h)).astype(BF16)

    @pl.when(j == 5)
    def _():
        for h in range(N_HEADS):
            vb_ref[h] = head(h).T.astype(BF16)


def _qkv_proj(x2, g, w_qkv, cos_t, sin_lo, sin_hi, seq, tm=512):
    t = x2.shape[0]
    n_rows = t // tm
    pos_blocks = seq // tm
    row_major = jax.ShapeDtypeStruct((N_HEADS, t, HEAD_DIM), BF16)
    col_major = jax.ShapeDtypeStruct((N_HEADS, HEAD_DIM, t), BF16)
    row_spec = pl.BlockSpec((N_HEADS, tm, HEAD_DIM), lambda i, j: (0, i, 0))
    col_spec = pl.BlockSpec((N_HEADS, HEAD_DIM, tm), lambda i, j: (0, 0, i))
    tab_spec = pl.BlockSpec((tm, HEAD_DIM), lambda i, j: (i % pos_blocks, 0))
    return pl.pallas_call(
        functools.partial(_qkv_kernel, tm=tm),
        grid=(n_rows, QKV_WIDTH // SECTION),
        in_specs=[
            pl.BlockSpec((tm, D_MODEL), lambda i, j: (i, 0)),
            pl.BlockSpec((1, D_MODEL), lambda i, j: (0, 0)),
            pl.BlockSpec((D_MODEL, SECTION), lambda i, j: (0, j)),
            tab_spec, tab_spec, tab_spec,
        ],
        out_specs=[
            col_spec, row_spec, pl.BlockSpec((N_HEADS, HEAD_DIM + ONES_ROWS, tm), lambda i, j: (0, 0, i)),
            col_spec, row_spec, col_spec,
            pl.BlockSpec((None, N_HEADS, tm // MOBA_BLOCK, HEAD_DIM), lambda i, j: (i, 0, 0, 0)),
        ],
        out_shape=[
            col_major, row_major, jax.ShapeDtypeStruct((N_HEADS, HEAD_DIM + ONES_ROWS, t), BF16),
            col_major, row_major, col_major,
            jax.ShapeDtypeStruct((n_rows, N_HEADS, tm // MOBA_BLOCK, HEAD_DIM), F32),
        ],
        scratch_shapes=[pltpu.VMEM((tm, D_MODEL), BF16)],
        compiler_params=_params(2),
        name="qkv_proj",
    )(x2, g, w_qkv, cos_t, sin_lo, sin_hi)


def _gate_kernel(x_ref, g_ref, w_ref, o_ref, u_ref):
    @pl.when(pl.program_id(1) == 0)
    def _():
        xf = x_ref[...]
        u_ref[...] = (xf * _rms_scale(xf) * g_ref[...]).astype(BF16)

    o_ref[...] = jax.nn.sigmoid(jnp.dot(u_ref[...], w_ref[...], preferred_element_type=F32))


def _gate_proj(x2, g, w_gate, tm=512, tn=1024):
    t = x2.shape[0]
    return pl.pallas_call(
        _gate_kernel,
        grid=(t // tm, GATE_WIDTH // tn),
        in_specs=[
            pl.BlockSpec((tm, D_MODEL), lambda i, j: (i, 0)),
            pl.BlockSpec((1, D_MODEL), lambda i, j: (0, 0)),
            pl.BlockSpec((D_MODEL, tn), lambda i, j: (0, j)),
        ],
        out_specs=pl.BlockSpec((tm, tn), lambda i, j: (i, j)),
        out_shape=jax.ShapeDtypeStruct((t, GATE_WIDTH), F32),
        scratch_shapes=[pltpu.VMEM((tm, D_MODEL), BF16)],
        compiler_params=_params(2),
        name="gate_proj",
    )(x2, g, w_gate)


def _softmax_start(s_t):
    m = jnp.max(s_t, axis=0, keepdims=True)
    p_t = jnp.exp2(s_t - m)
    return m, jnp.sum(p_t, axis=0, keepdims=True), p_t.astype(BF16)


def _softmax_step(s_t, m, l):
    m_new = jnp.maximum(m, jnp.max(s_t, axis=0, keepdims=True))
    alpha = jnp.exp2(m - m_new)
    p_t = jnp.exp2(s_t - m_new)
    return m_new, alpha * l + jnp.sum(p_t, axis=0, keepdims=True), alpha, p_t.astype(BF16)


def _causal_bias():
    key = lax.broadcasted_iota(jnp.int32, (ATTN_TQ, ATTN_TQ), 0)
    qry = lax.broadcasted_iota(jnp.int32, (ATTN_TQ, ATTN_TQ), 1)
    return jnp.where(key <= qry, 0.0, NEG_INF)


def _moba_kernel(q_ref, k_ref, v_ref, kmean_ref, o_ref, acc_ref, s0_ref, s1_ref, s2_ref, s3_ref, *, n_blocks):
    sup = pl.program_id(2)
    q_all = q_ref[...]

    km = kmean_ref[...]
    km_hi = km.astype(BF16)
    km_mid = (km - km_hi.astype(F32)).astype(BF16)
    km_lo = (km - km_hi.astype(F32) - km_mid.astype(F32)).astype(BF16)
    gate = (jnp.dot(km_lo, q_all, preferred_element_type=F32) + jnp.dot(km_mid, q_all, preferred_element_type=F32)
            + jnp.dot(km_hi, q_all, preferred_element_type=F32))
    blk = lax.broadcasted_iota(jnp.int32, gate.shape, 0).astype(F32)
    own_blk = (sup * TILES_PER_STEP
               + lax.broadcasted_iota(jnp.int32, (1, ATTN_CHUNK), 1) // MOBA_BLOCK).astype(F32)
    g = jnp.where(blk < own_blk, gate, NEG_INF)
    picks = []
    for _ in range(MOBA_TOPK):
        best = jnp.max(g, axis=0, keepdims=True)
        first = jnp.min(jnp.where(g == best, blk, float(n_blocks)), axis=0, keepdims=True)
        first = jnp.where(best > NEG_INF, first, -1.0)
        picks.append(first)
        g = jnp.where(blk == first, NEG_INF, g)

    def lanes(a):
        return slice(a * ATTN_TQ, (a + 1) * ATTN_TQ)

    q_t = [q_all[:, lanes(a)] for a in range(TILES_PER_STEP)]
    tile_picks = [[p[:, lanes(a)] for p in picks] for a in range(TILES_PER_STEP)]
    s_refs = (s0_ref, s1_ref, s2_ref, s3_ref)
    causal_bias = _causal_bias()

    def selection_bias(a, block_index):
        jf = jnp.asarray(block_index).astype(F32)
        chosen = (tile_picks[a][0] == jf) | (tile_picks[a][1] == jf) | (tile_picks[a][2] == jf)
        return jnp.where(chosen, 0.0, NEG_INF)

    def past_biases(a, c):
        return [selection_bias(a, c * BLOCKS_PER_CHUNK + r) for r in range(BLOCKS_PER_CHUNK)]

    def own_biases(a):
        return [selection_bias(a, sup * TILES_PER_STEP + r) for r in range(a)] + [causal_bias]

    def scores_pass(a, start, biases):
        top = None
        for r, bias in enumerate(biases):
            k_r = k_ref[pl.ds(start + r * MOBA_BLOCK, MOBA_BLOCK), :]
            s_r = jnp.dot(k_r, q_t[a], preferred_element_type=F32) + bias
            s_refs[a][r * MOBA_BLOCK:(r + 1) * MOBA_BLOCK, :] = s_r
            part = jnp.max(s_r.reshape(MOBA_BLOCK // 8, 8, ATTN_TQ), axis=0)
            top = part if top is None else jnp.maximum(top, part)
        return jnp.max(top, axis=0, keepdims=True)

    def values_pass(a, start, rows, m_new, alpha):
        p_t = jnp.exp2(s_refs[a][:rows, :] - m_new).astype(BF16)
        update = jnp.dot(v_ref[:, pl.ds(start, rows)], p_t, preferred_element_type=F32)
        acc_ref[a] = update if alpha is None else alpha * acc_ref[a] + update

    own = pl.multiple_of(sup * ATTN_CHUNK, ATTN_CHUNK)
    m_chunk = scores_pass(0, own, own_biases(0))
    tops = []
    for a in range(TILES_PER_STEP):
        if a + 1 < TILES_PER_STEP:
            m_next = scores_pass(a + 1, own, own_biases(a + 1))
        else:
            m_next = scores_pass(0, 0, past_biases(0, 0))
        values_pass(a, own, (a + 1) * MOBA_BLOCK, m_chunk, None)
        tops.append(m_chunk)
        m_chunk = m_next

    def body(c, carry):
        tops, m_chunk = carry
        start = pl.multiple_of(c * ATTN_CHUNK, ATTN_CHUNK)
        out = []
        for a in range(TILES_PER_STEP):
            if a + 1 < TILES_PER_STEP:
                m_next = scores_pass(a + 1, start, past_biases(a + 1, c))
            else:
                m_next = scores_pass(0, pl.multiple_of((c + 1) * ATTN_CHUNK, ATTN_CHUNK), past_biases(0, c + 1))
            m_new = jnp.maximum(tops[a], m_chunk)
            values_pass(a, start, ATTN_CHUNK, m_new, jnp.exp2(tops[a] - m_new))
            out.append(m_new)
            m_chunk = m_next
        return tuple(out), m_chunk

    lax.fori_loop(0, sup, body, (tuple(tops), m_chunk))
    for a in range(TILES_PER_STEP):
        acc = acc_ref[a]
        o_ref[lanes(a), :] = (acc[:HEAD_DIM] / acc[HEAD_DIM:HEAD_DIM + 1]).T.astype(o_ref.dtype)


def _moba_attention(qa_t, ka, va_t, kmean, batch, seq):
    t = batch * seq
    n_blocks = seq // MOBA_BLOCK
    n_steps = seq // ATTN_CHUNK
    return pl.pallas_call(
        functools.partial(_moba_kernel, n_blocks=n_blocks),
        grid=(batch, N_HEADS, n_steps),
        in_specs=[
            pl.BlockSpec((None, HEAD_DIM, ATTN_CHUNK), lambda b, h, i: (h, 0, b * n_steps + i)),
            pl.BlockSpec((None, seq, HEAD_DIM), lambda b, h, i: (h, b, 0)),
            pl.BlockSpec((None, HEAD_DIM + ONES_ROWS, seq), lambda b, h, i: (h, 0, b)),
            pl.BlockSpec((None, n_blocks, HEAD_DIM), lambda b, h, i: (h, b, 0)),
        ],
        out_specs=pl.BlockSpec((ATTN_CHUNK, HEAD_DIM), lambda b, h, i: (b * n_steps + i, h)),
        out_shape=jax.ShapeDtypeStruct((t, SECTION), BF16),
        scratch_shapes=[pltpu.VMEM((TILES_PER_STEP, HEAD_DIM + ONES_ROWS, ATTN_TQ), F32)]
                       + [pltpu.VMEM((ATTN_CHUNK, ATTN_TQ), F32)] * TILES_PER_STEP,
        compiler_params=_params(3),
        name="moba_attn",
    )(qa_t, ka, va_t, kmean)


def _diff_kernel(q0_ref, q1_ref, k0_ref, k1_ref, vlo_ref, vhi_ref, lq1_ref, lk1_ref, lq2_ref, lk2_ref, subg_ref,
                 o_ref, acc_ref, *s_refs, lambda_init):
    sup = pl.program_id(2)
    q_refs = (q0_ref, q1_ref)
    k_refs = (k0_ref, k1_ref)
    chains = [(a, sub) for a in range(TILES_PER_STEP) for sub in range(2)]
    n_chains = len(chains)
    q_t = [q_refs[sub][:, a * ATTN_TQ:(a + 1) * ATTN_TQ] for a, sub in chains]
    causal_bias = _causal_bias()

    def scores_pass(n, start, n_blocks, causal_last):
        top = None
        for r in range(n_blocks):
            k_r = k_refs[chains[n][1]][pl.ds(start + r * ATTN_TQ, ATTN_TQ), :]
            s_r = jnp.dot(k_r, q_t[n], preferred_element_type=F32)
            if causal_last and r == n_blocks - 1:
                s_r = s_r + causal_bias
            s_refs[n][r * ATTN_TQ:(r + 1) * ATTN_TQ, :] = s_r
            part = jnp.max(s_r.reshape(ATTN_TQ // 8, 8, ATTN_TQ), axis=0)
            top = part if top is None else jnp.maximum(top, part)
        return jnp.max(top, axis=0, keepdims=True)

    def values_pass(n, start, rows, m_new, alpha, l):
        p_t = jnp.exp2(s_refs[n][:rows, :] - m_new)
        l_chunk = jnp.sum(jnp.sum(p_t.reshape(rows // 8, 8, ATTN_TQ), axis=0), axis=0, keepdims=True)
        v_t = jnp.concatenate([vlo_ref[:, pl.ds(start, rows)], vhi_ref[:, pl.ds(start, rows)]], axis=0)
        update = jnp.dot(v_t, p_t.astype(BF16), preferred_element_type=F32)
        if alpha is None:
            acc_ref[n] = update
            return l_chunk
        acc_ref[n] = alpha * acc_ref[n] + update
        return alpha * l + l_chunk

    own = pl.multiple_of(sup * ATTN_CHUNK, ATTN_CHUNK)
    m_chunk = scores_pass(0, own, 1, True)
    stats = []
    for n, (a, sub) in enumerate(chains):
        if n + 1 < n_chains:
            m_next = scores_pass(n + 1, own, chains[n + 1][0] + 1, True)
        else:
            m_next = scores_pass(0, 0, BLOCKS_PER_CHUNK, False)
        stats.append((m_chunk, values_pass(n, own, (a + 1) * ATTN_TQ, m_chunk, None, None)))
        m_chunk = m_next

    def body(c, carry):
        stats, m_chunk = carry
        start = pl.multiple_of(c * ATTN_CHUNK, ATTN_CHUNK)
        out = []
        for n in range(n_chains):
            if n + 1 < n_chains:
                m_next = scores_pass(n + 1, start, BLOCKS_PER_CHUNK, False)
            else:
                m_next = scores_pass(0, pl.multiple_of((c + 1) * ATTN_CHUNK, ATTN_CHUNK), BLOCKS_PER_CHUNK, False)
            m, l = stats[n]
            m_new = jnp.maximum(m, m_chunk)
            out.append((m_new, values_pass(n, start, ATTN_CHUNK, m_new, jnp.exp2(m - m_new), l)))
            m_chunk = m_next
        return tuple(out), m_chunk

    stats, _ = lax.fori_loop(0, sup, body, (tuple(stats), m_chunk))

    lam = (jnp.exp(jnp.sum(lq1_ref[...] * lk1_ref[...], axis=1, keepdims=True))
           - jnp.exp(jnp.sum(lq2_ref[...] * lk2_ref[...], axis=1, keepdims=True)) + lambda_init)
    for a in range(TILES_PER_STEP):
        out = acc_ref[2 * a] / stats[2 * a][1] - lam * (acc_ref[2 * a + 1] / stats[2 * a + 1][1])
        inv = lax.rsqrt(jnp.mean(out * out, axis=0, keepdims=True) + NORM_EPS)
        out = out * inv * subg_ref[...]
        o_ref[a * ATTN_TQ:(a + 1) * ATTN_TQ, :] = (out * (1.0 - lambda_init)).T.astype(o_ref.dtype)


def _diff_attention(qb_t, kb, vb_t, lq1, lk1, lq2, lk2, sub_g_col, batch, seq, lambda_init):
    t = batch * seq
    n_steps = seq // ATTN_CHUNK
    q_spec = lambda c: pl.BlockSpec((None, HEAD_DIM, ATTN_CHUNK), lambda b, h, i: (2 * h + c, 0, b * n_steps + i))
    k_spec = lambda c: pl.BlockSpec((None, seq, HEAD_DIM), lambda b, h, i: (2 * h + c, b, 0))
    v_spec = lambda c: pl.BlockSpec((None, HEAD_DIM, seq), lambda b, h, i: (2 * h + c, 0, b))
    vec_spec = pl.BlockSpec((1, HEAD_DIM), lambda b, h, i: (0, 0))
    return pl.pallas_call(
        functools.partial(_diff_kernel, lambda_init=lambda_init),
        grid=(batch, DIFF_HEADS, n_steps),
        in_specs=[
            q_spec(0), q_spec(1), k_spec(0), k_spec(1), v_spec(0), v_spec(1),
            vec_spec, vec_spec, vec_spec, vec_spec,
            pl.BlockSpec((DIFF_V_DIM, 1), lambda b, h, i: (0, 0)),
        ],
        out_specs=pl.BlockSpec((ATTN_CHUNK, DIFF_V_DIM), lambda b, h, i: (b * n_steps + i, h)),
        out_shape=jax.ShapeDtypeStruct((t, DIFF_HEADS * DIFF_V_DIM), BF16),
        scratch_shapes=[pltpu.VMEM((2 * TILES_PER_STEP, DIFF_V_DIM, ATTN_TQ), F32)]
                       + [pltpu.VMEM((ATTN_CHUNK, ATTN_TQ), F32)] * (2 * TILES_PER_STEP),
        compiler_params=_params(3),
        name="diff_attn",
    )(qb_t, qb_t, kb, kb, vb_t, vb_t, lq1, lk1, lq2, lk2, sub_g_col)


def _mix_kernel(oa_ref, ob_ref, sga_ref, sgb_ref, x_ref, wm_ref, wd_ref, wo_ref, g_ref, o_ref):
    ya = jnp.dot(oa_ref[...], wm_ref[...], preferred_element_type=F32)
    yb = jnp.dot(ob_ref[...], wd_ref[...], preferred_element_type=F32)
    mixed = sga_ref[...] * ya + sgb_ref[...] * yb
    z = jnp.dot(mixed.astype(BF16), wo_ref[...], preferred_element_type=F32)
    o_ref[...] = x_ref[...] + z * _rms_scale(z) * g_ref[...]


def _mix(oa, ob, gates, x2, wm, wd, wo, g, tm=256):
    t = x2.shape[0]
    const = lambda shape: pl.BlockSpec(shape, lambda i: (0, 0))
    return pl.pallas_call(
        _mix_kernel,
        grid=(t // tm,),
        in_specs=[
            pl.BlockSpec((tm, SECTION), lambda i: (i, 0)),
            pl.BlockSpec((tm, SECTION), lambda i: (i, 0)),
            pl.BlockSpec((tm, D_MODEL), lambda i: (i, 0)),
            pl.BlockSpec((tm, D_MODEL), lambda i: (i, 1)),
            pl.BlockSpec((tm, D_MODEL), lambda i: (i, 0)),
            const((SECTION, D_MODEL)), const((SECTION, D_MODEL)), const((D_MODEL, D_MODEL)),
            const((1, D_MODEL)),
        ],
        out_specs=pl.BlockSpec((tm, D_MODEL), lambda i: (i, 0)),
        out_shape=jax.ShapeDtypeStruct((t, D_MODEL), F32),
        compiler_params=_params(1),
        name="mix_out",
    )(oa, ob, gates, gates, x2, wm, wd, wo, g)


def _mlp_kernel(h_ref, gpre_ref, wup_ref, wdown_ref, gpost_ref, o_ref, u_ref, acc_ref):
    k = pl.program_id(1)

    @pl.when(k == 0)
    def _():
        hf = h_ref[...]
        u_ref[...] = (hf * _rms_scale(hf) * gpre_ref[...]).astype(BF16)

    a = jnp.dot(u_ref[...], wup_ref[...], preferred_element_type=F32)
    a = jnp.square(jnp.maximum(a, 0.0)).astype(BF16)
    part = jnp.dot(a, wdown_ref[...], preferred_element_type=F32)

    @pl.when(k == 0)
    def _():
        acc_ref[...] = part

    @pl.when(k > 0)
    def _():
        acc_ref[...] += part

    @pl.when(k == pl.num_programs(1) - 1)
    def _():
        ff = acc_ref[...]
        o_ref[...] = h_ref[...] + ff * _rms_scale(ff) * gpost_ref[...]


def _mlp(h, gpre, wup, wdown, gpost, tm=512, tf=1024):
    t = h.shape[0]
    return pl.pallas_call(
        _mlp_kernel,
        grid=(t // tm, D_FF // tf),
        in_specs=[
            pl.BlockSpec((tm, D_MODEL), lambda i, k: (i, 0)),
            pl.BlockSpec((1, D_MODEL), lambda i, k: (0, 0)),
            pl.BlockSpec((D_MODEL, tf), lambda i, k: (0, k)),
            pl.BlockSpec((tf, D_MODEL), lambda i, k: (k, 0)),
            pl.BlockSpec((1, D_MODEL), lambda i, k: (0, 0)),
        ],
        out_specs=pl.BlockSpec((tm, D_MODEL), lambda i, k: (i, 0)),
        out_shape=jax.ShapeDtypeStruct((t, D_MODEL), F32),
        scratch_shapes=[pltpu.VMEM((tm, D_MODEL), BF16), pltpu.VMEM((tm, D_MODEL), F32)],
        compiler_params=_params(2),
        name="mlp",
    )(h, gpre, wup, wdown, gpost)


def _ple_kernel(h_ref, p_ref, gpre_ref, wgate_ref, wproj_ref, gpost_ref, o_ref):
    hf = h_ref[...]
    u = (hf * _rms_scale(hf) * gpre_ref[...]).astype(BF16)
    gate = jax.nn.sigmoid(jnp.dot(u, wgate_ref[...], preferred_element_type=F32))
    e = jnp.dot(p_ref[...].astype(BF16), wproj_ref[...], preferred_element_type=F32) * gate
    o_ref[...] = hf + e * _rms_scale(e) * gpost_ref[...]


def _ple(h, p2, gpre, wgate, wproj, gpost, tm=512):
    t = h.shape[0]
    const = lambda shape: pl.BlockSpec(shape, lambda i: (0, 0))
    return pl.pallas_call(
        _ple_kernel,
        grid=(t // tm,),
        in_specs=[
            pl.BlockSpec((tm, D_MODEL), lambda i: (i, 0)),
            pl.BlockSpec((tm, PLE_DIM), lambda i: (i, 0)),
            const((1, D_MODEL)), const((D_MODEL, D_MODEL)), const((PLE_DIM, D_MODEL)), const((1, D_MODEL)),
        ],
        out_specs=pl.BlockSpec((tm, D_MODEL), lambda i: (i, 0)),
        out_shape=jax.ShapeDtypeStruct((t, D_MODEL), F32),
        compiler_params=_params(1),
        name="ple",
    )(h, p2, gpre, wgate, wproj, gpost)


def _rope_tables(seq):
    half = ROT_DIM // 2
    pos_f = jnp.arange(seq, dtype=F32)
    inv_freq = 1.0 / (ROPE_THETA ** (jnp.arange(half, dtype=F32) * 2.0 / ROT_DIM))
    ang = pos_f[:, None] * inv_freq[None, :]
    cos, sin = jnp.cos(ang), jnp.sin(ang)
    pad = HEAD_DIM - ROT_DIM
    zeros = jnp.zeros((seq, half), F32)
    cos_t = jnp.concatenate([cos, cos, jnp.ones((seq, pad), F32)], axis=1)
    sin_lo = jnp.concatenate([-sin, zeros, jnp.zeros((seq, pad), F32)], axis=1)
    sin_hi = jnp.concatenate([zeros, sin, jnp.zeros((seq, pad), F32)], axis=1)
    return cos_t, sin_lo, sin_hi


def kernel(x, p, w_in, w_br_moba, w_br_diff, w_out, lambda_q1, lambda_k1, lambda_q2, lambda_k2, diff_subln_g,
           g_mix_pre, g_mix_post, w_up, w_down, g_mlp_pre, g_mlp_post, w_ple_proj, w_ple_gate, g_ple_pre,
           g_ple_post):
    batch, seq, _ = x.shape
    depth = w_in.shape[0]
    t = batch * seq
    n_blocks = seq // MOBA_BLOCK
    cos_t, sin_lo, sin_hi = _rope_tables(seq)
    row = lambda v: v.reshape(1, -1).astype(F32)

    h = x.reshape(t, D_MODEL)
    for layer in range(depth):
        lambda_init = 0.8 - 0.6 * math.exp(-0.3 * layer)
        w_l = w_in[layer].astype(BF16)
        qa_t, ka, va_t, qb_t, kb, vb_t, kmean = _qkv_proj(h, row(g_mix_pre[layer]), w_l[:, :QKV_WIDTH],
                                                          cos_t, sin_lo, sin_hi, seq)
        gates = _gate_proj(h, row(g_mix_pre[layer]), w_l[:, QKV_WIDTH:])
        kmean = kmean.transpose(1, 0, 2, 3).reshape(N_HEADS, batch * n_blocks, HEAD_DIM)
        oa = _moba_attention(qa_t, ka, va_t, kmean, batch, seq)
        ob = _diff_attention(qb_t, kb, vb_t, row(lambda_q1[layer]), row(lambda_k1[layer]), row(lambda_q2[layer]),
                             row(lambda_k2[layer]), diff_subln_g[layer].reshape(-1, 1).astype(F32),
                             batch, seq, lambda_init)
        h = _mix(oa, ob, gates, h, w_br_moba[layer].astype(BF16), w_br_diff[layer].astype(BF16),
                 w_out[layer].astype(BF16), row(g_mix_post[layer]))
        h = _mlp(h, row(g_mlp_pre[layer]), w_up[layer].astype(BF16), w_down[layer].astype(BF16),
                 row(g_mlp_post[layer]))
        h = _ple(h, p[layer].reshape(t, PLE_DIM), row(g_ple_pre[layer]), w_ple_gate[layer].astype(BF16),
                 w_ple_proj[layer].astype(BF16), row(g_ple_post[layer]))
    return h.reshape(batch, seq, D_MODEL)
```

```python
import functools
import math

import jax
import jax.numpy as jnp
from jax import lax
from jax.experimental import pallas as pl
from jax.experimental.pallas import tpu as pltpu

F32 = jnp.float32
BF16 = jnp.bfloat16

D_MODEL = 2048
HEAD_DIM = 128
N_HEADS = 8
MOBA_BLOCK = 256
MOBA_TOPK = 3
DIFF_HEADS = 4
DIFF_V_DIM = 2 * HEAD_DIM
ROT_DIM = HEAD_DIM // 4
ROPE_THETA = 500000.0
D_FF = 4 * D_MODEL
PLE_DIM = 256
NORM_EPS = 1e-6
SECTION = N_HEADS * HEAD_DIM
QKV_WIDTH = 6 * SECTION
GATE_WIDTH = 2 * D_MODEL

ATTN_TQ = 256
ATTN_CHUNK = 1024
BLOCKS_PER_CHUNK = ATTN_CHUNK // MOBA_BLOCK
TILES_PER_STEP = ATTN_CHUNK // ATTN_TQ
ONES_ROWS = 16

VMEM_LIMIT_BYTES = 56 * 1024 * 1024

NEG_INF = float("-inf")
NT_DIMS = (((1,), (1,)), ((), ()))
Q_SCALE = HEAD_DIM ** -0.5 * math.log2(math.e)


def _params(n_axes):
    return pltpu.CompilerParams(dimension_semantics=("arbitrary",) * n_axes,
                                vmem_limit_bytes=VMEM_LIMIT_BYTES)


def _rms_scale(xf):
    return lax.rsqrt(jnp.mean(xf * xf, axis=-1, keepdims=True) + NORM_EPS)


def _qkv_kernel(x_ref, g_ref, w_ref, cos_ref, sin_lo_ref, sin_hi_ref,
                qa_ref, ka_ref, va_ref, qb_ref, kb_ref, vb_ref, kmean_ref, *, tm):
    xf = x_ref[...]
    u = (xf * _rms_scale(xf) * g_ref[...]).astype(BF16)
    cos, sin_lo, sin_hi = cos_ref[...], sin_lo_ref[...], sin_hi_ref[...]

    def section(j):
        y = jnp.dot(u, w_ref[:, j * SECTION:(j + 1) * SECTION], preferred_element_type=F32)
        return [y[:, h * HEAD_DIM:(h + 1) * HEAD_DIM] for h in range(N_HEADS)]

    def rope(yh):
        from_hi = pltpu.roll(yh, HEAD_DIM - ROT_DIM // 2, axis=1)
        from_lo = pltpu.roll(yh, ROT_DIM // 2, axis=1)
        return yh * cos + from_hi * sin_lo + from_lo * sin_hi

    for h, yh in enumerate(section(0)):
        qa_ref[h] = (rope(yh) * Q_SCALE).T.astype(BF16)
    for h, yh in enumerate(section(1)):
        kr = rope(yh)
        ka_ref[h] = kr.astype(BF16)
        kmean_ref[h] = jnp.mean(kr.reshape(tm // MOBA_BLOCK, MOBA_BLOCK, HEAD_DIM), axis=1)
    ones_tile = jnp.where(lax.broadcasted_iota(jnp.int32, (ONES_ROWS, tm), 0) == 0, 1.0, 0.0).astype(BF16)
    for h, yh in enumerate(section(2)):
        va_ref[h, :HEAD_DIM] = yh.T.astype(BF16)
        va_ref[h, HEAD_DIM:] = ones_tile
    for h, yh in enumerate(section(3)):
        qb_ref[h] = (rope(yh) * Q_SCALE).T.astype(BF16)
    for h, yh in enumerate(section(4)):
        kb_ref[h] = rope(yh).astype(BF16)
    for h, yh in enumerate(section(5)):
        vb_ref[h] = yh.T.astype(BF16)


def _qkv_proj(x2, g, w_qkv, cos_t, sin_lo, sin_hi, seq, tm=512):
    t = x2.shape[0]
    n_rows = t // tm
    pos_blocks = seq // tm
    row_major = jax.ShapeDtypeStruct((N_HEADS, t, HEAD_DIM), BF16)
    col_major = jax.ShapeDtypeStruct((N_HEADS, HEAD_DIM, t), BF16)
    row_spec = pl.BlockSpec((N_HEADS, tm, HEAD_DIM), lambda i: (0, i, 0))
    col_spec = pl.BlockSpec((N_HEADS, HEAD_DIM, tm), lambda i: (0, 0, i))
    tab_spec = pl.BlockSpec((tm, HEAD_DIM), lambda i: (i % pos_blocks, 0))
    return pl.pallas_call(
        functools.partial(_qkv_kernel, tm=tm),
        grid=(n_rows,),
        in_specs=[
            pl.BlockSpec((tm, D_MODEL), lambda i: (i, 0)),
            pl.BlockSpec((1, D_MODEL), lambda i: (0, 0)),
            pl.BlockSpec((D_MODEL, QKV_WIDTH), lambda i: (0, 0), pipeline_mode=pl.Buffered(1)),
            tab_spec, tab_spec, tab_spec,
        ],
        out_specs=[
            col_spec, row_spec, pl.BlockSpec((N_HEADS, HEAD_DIM + ONES_ROWS, tm), lambda i: (0, 0, i)),
            col_spec, row_spec, col_spec,
            pl.BlockSpec((None, N_HEADS, tm // MOBA_BLOCK, HEAD_DIM), lambda i: (i, 0, 0, 0)),
        ],
        out_shape=[
            col_major, row_major, jax.ShapeDtypeStruct((N_HEADS, HEAD_DIM + ONES_ROWS, t), BF16),
            col_major, row_major, col_major,
            jax.ShapeDtypeStruct((n_rows, N_HEADS, tm // MOBA_BLOCK, HEAD_DIM), F32),
        ],
        compiler_params=_params(1),
        name="qkv_proj",
    )(x2, g, w_qkv, cos_t, sin_lo, sin_hi)


def _gate_kernel(x_ref, g_ref, w_ref, o_ref, *, tn):
    xf = x_ref[...]
    u = (xf * _rms_scale(xf) * g_ref[...]).astype(BF16)
    for j in range(GATE_WIDTH // tn):
        cols = slice(j * tn, (j + 1) * tn)
        o_ref[:, cols] = jax.nn.sigmoid(jnp.dot(u, w_ref[:, cols], preferred_element_type=F32))


def _gate_proj(x2, g, w_gate, tm=512, tn=1024):
    t = x2.shape[0]
    return pl.pallas_call(
        functools.partial(_gate_kernel, tn=tn),
        grid=(t // tm,),
        in_specs=[
            pl.BlockSpec((tm, D_MODEL), lambda i: (i, 0)),
            pl.BlockSpec((1, D_MODEL), lambda i: (0, 0)),
            pl.BlockSpec((D_MODEL, GATE_WIDTH), lambda i: (0, 0), pipeline_mode=pl.Buffered(1)),
        ],
        out_specs=pl.BlockSpec((tm, GATE_WIDTH), lambda i: (i, 0)),
        out_shape=jax.ShapeDtypeStruct((t, GATE_WIDTH), F32),
        compiler_params=_params(1),
        name="gate_proj",
    )(x2, g, w_gate)


def _softmax_start(s_t):
    m = jnp.max(s_t, axis=0, keepdims=True)
    p_t = jnp.exp2(s_t - m)
    return m, jnp.sum(p_t, axis=0, keepdims=True), p_t.astype(BF16)


def _softmax_step(s_t, m, l):
    m_new = jnp.maximum(m, jnp.max(s_t, axis=0, keepdims=True))
    alpha = jnp.exp2(m - m_new)
    p_t = jnp.exp2(s_t - m_new)
    return m_new, alpha * l + jnp.sum(p_t, axis=0, keepdims=True), alpha, p_t.astype(BF16)


def _causal_bias():
    key = lax.broadcasted_iota(jnp.int32, (ATTN_TQ, ATTN_TQ), 0)
    qry = lax.broadcasted_iota(jnp.int32, (ATTN_TQ, ATTN_TQ), 1)
    return jnp.where(key <= qry, 0.0, NEG_INF)


def _moba_kernel(q_ref, k_ref, v_ref, kmean_ref, o_ref, acc_ref, s0_ref, s1_ref, s2_ref, s3_ref, *, n_blocks):
    sup = pl.program_id(2)
    q_all = q_ref[...]

    km = kmean_ref[...]
    km_hi = km.astype(BF16)
    km_mid = (km - km_hi.astype(F32)).astype(BF16)
    km_lo = (km - km_hi.astype(F32) - km_mid.astype(F32)).astype(BF16)
    gate = (jnp.dot(km_lo, q_all, preferred_element_type=F32) + jnp.dot(km_mid, q_all, preferred_element_type=F32)
            + jnp.dot(km_hi, q_all, preferred_element_type=F32))
    blk = lax.broadcasted_iota(jnp.int32, gate.shape, 0).astype(F32)
    own_blk = (sup * TILES_PER_STEP
               + lax.broadcasted_iota(jnp.int32, (1, ATTN_CHUNK), 1) // MOBA_BLOCK).astype(F32)
    g = jnp.where(blk < own_blk, gate, NEG_INF)
    picks = []
    for _ in range(MOBA_TOPK):
        best = jnp.max(g, axis=0, keepdims=True)
        first = jnp.min(jnp.where(g == best, blk, float(n_blocks)), axis=0, keepdims=True)
        first = jnp.where(best > NEG_INF, first, -1.0)
        picks.append(first)
        g = jnp.where(blk == first, NEG_INF, g)

    def lanes(a):
        return slice(a * ATTN_TQ, (a + 1) * ATTN_TQ)

    q_t = [q_all[:, lanes(a)] for a in range(TILES_PER_STEP)]
    tile_picks = [[p[:, lanes(a)] for p in picks] for a in range(TILES_PER_STEP)]
    s_refs = (s0_ref, s1_ref, s2_ref, s3_ref)
    causal_bias = _causal_bias()

    def selection_bias(a, block_index):
        jf = jnp.asarray(block_index).astype(F32)
        chosen = (tile_picks[a][0] == jf) | (tile_picks[a][1] == jf) | (tile_picks[a][2] == jf)
        return jnp.where(chosen, 0.0, NEG_INF)

    def past_biases(a, c):
        return [selection_bias(a, c * BLOCKS_PER_CHUNK + r) for r in range(BLOCKS_PER_CHUNK)]

    def own_biases(a):
        return [selection_bias(a, sup * TILES_PER_STEP + r) for r in range(a)] + [causal_bias]

    def scores_pass(a, start, biases):
        top = None
        for r, bias in enumerate(biases):
            k_r = k_ref[pl.ds(start + r * MOBA_BLOCK, MOBA_BLOCK), :]
            s_r = jnp.dot(k_r, q_t[a], preferred_element_type=F32) + bias
            s_refs[a][r * MOBA_BLOCK:(r + 1) * MOBA_BLOCK, :] = s_r
            part = jnp.max(s_r.reshape(MOBA_BLOCK // 8, 8, ATTN_TQ), axis=0)
            top = part if top is None else jnp.maximum(top, part)
        return jnp.max(top, axis=0, keepdims=True)

    def values_pass(a, start, rows, m_new, alpha):
        p_t = jnp.exp2(s_refs[a][:rows, :] - m_new).astype(BF16)
        update = jnp.dot(v_ref[:, pl.ds(start, rows)], p_t, preferred_element_type=F32)
        acc_ref[a] = update if alpha is None else alpha * acc_ref[a] + update

    own = pl.multiple_of(sup * ATTN_CHUNK, ATTN_CHUNK)
    m_chunk = scores_pass(0, own, own_biases(0))
    tops = []
    for a in range(TILES_PER_STEP):
        if a + 1 < TILES_PER_STEP:
            m_next = scores_pass(a + 1, own, own_biases(a + 1))
        else:
            m_next = scores_pass(0, 0, past_biases(0, 0))
        values_pass(a, own, (a + 1) * MOBA_BLOCK, m_chunk, None)
        tops.append(m_chunk)
        m_chunk = m_next

    def body(c, carry):
        tops, m_chunk = carry
        start = pl.multiple_of(c * ATTN_CHUNK, ATTN_CHUNK)
        out = []
        for a in range(TILES_PER_STEP):
            if a + 1 < TILES_PER_STEP:
                m_next = scores_pass(a + 1, start, past_biases(a + 1, c))
            else:
                m_next = scores_pass(0, pl.multiple_of((c + 1) * ATTN_CHUNK, ATTN_CHUNK), past_biases(0, c + 1))
            m_new = jnp.maximum(tops[a], m_chunk)
            values_pass(a, start, ATTN_CHUNK, m_new, jnp.exp2(tops[a] - m_new))
            out.append(m_new)
            m_chunk = m_next
        return tuple(out), m_chunk

    lax.fori_loop(0, sup, body, (tuple(tops), m_chunk))
    for a in range(TILES_PER_STEP):
        acc = acc_ref[a]
        o_ref[lanes(a), :] = (acc[:HEAD_DIM] / acc[HEAD_DIM:HEAD_DIM + 1]).T.astype(o_ref.dtype)


def _moba_attention(qa_t, ka, va_t, kmean, batch, seq):
    t = batch * seq
    n_blocks = seq // MOBA_BLOCK
    n_steps = seq // ATTN_CHUNK
    return pl.pallas_call(
        functools.partial(_moba_kernel, n_blocks=n_blocks),
        grid=(batch, N_HEADS, n_steps),
        in_specs=[
            pl.BlockSpec((None, HEAD_DIM, ATTN_CHUNK), lambda b, h, i: (h, 0, b * n_steps + i)),
            pl.BlockSpec((None, seq, HEAD_DIM), lambda b, h, i: (h, b, 0)),
            pl.BlockSpec((None, HEAD_DIM + ONES_ROWS, seq), lambda b, h, i: (h, 0, b)),
            pl.BlockSpec((None, n_blocks, HEAD_DIM), lambda b, h, i: (h, b, 0)),
        ],
        out_specs=pl.BlockSpec((ATTN_CHUNK, HEAD_DIM), lambda b, h, i: (b * n_steps + i, h)),
        out_shape=jax.ShapeDtypeStruct((t, SECTION), BF16),
        scratch_shapes=[pltpu.VMEM((TILES_PER_STEP, HEAD_DIM + ONES_ROWS, ATTN_TQ), F32)]
                       + [pltpu.VMEM((ATTN_CHUNK, ATTN_TQ), F32)] * TILES_PER_STEP,
        compiler_params=_params(3),
        name="moba_attn",
    )(qa_t, ka, va_t, kmean)


def _diff_kernel(q0_ref, q1_ref, k0_ref, k1_ref, vlo_ref, vhi_ref, lq1_ref, lk1_ref, lq2_ref, lk2_ref, subg_ref,
                 o_ref, acc_ref, *s_refs, lambda_init):
    sup = pl.program_id(2)
    q_refs = (q0_ref, q1_ref)
    k_refs = (k0_ref, k1_ref)
    chains = [(a, sub) for a in range(TILES_PER_STEP) for sub in range(2)]
    n_chains = len(chains)
    q_t = [q_refs[sub][:, a * ATTN_TQ:(a + 1) * ATTN_TQ] for a, sub in chains]
    causal_bias = _causal_bias()

    def scores_pass(n, start, n_blocks, causal_last):
        top = None
        for r in range(n_blocks):
            k_r = k_refs[chains[n][1]][pl.ds(start + r * ATTN_TQ, ATTN_TQ), :]
            s_r = jnp.dot(k_r, q_t[n], preferred_element_type=F32)
            if causal_last and r == n_blocks - 1:
                s_r = s_r + causal_bias
            s_refs[n][r * ATTN_TQ:(r + 1) * ATTN_TQ, :] = s_r
            part = jnp.max(s_r.reshape(ATTN_TQ // 8, 8, ATTN_TQ), axis=0)
            top = part if top is None else jnp.maximum(top, part)
        return jnp.max(top, axis=0, keepdims=True)

    def values_pass(n, start, rows, m_new, alpha, l):
        p_t = jnp.exp2(s_refs[n][:rows, :] - m_new)
        l_chunk = jnp.sum(jnp.sum(p_t.reshape(rows // 8, 8, ATTN_TQ), axis=0), axis=0, keepdims=True)
        v_t = jnp.concatenate([vlo_ref[:, pl.ds(start, rows)], vhi_ref[:, pl.ds(start, rows)]], axis=0)
        update = jnp.dot(v_t, p_t.astype(BF16), preferred_element_type=F32)
        if alpha is None:
            acc_ref[n] = update
            return l_chunk
        acc_ref[n] = alpha * acc_ref[n] + update
        return alpha * l + l_chunk

    own = pl.multiple_of(sup * ATTN_CHUNK, ATTN_CHUNK)
    m_chunk = scores_pass(0, own, 1, True)
    stats = []
    for n, (a, sub) in enumerate(chains):
        if n + 1 < n_chains:
            m_next = scores_pass(n + 1, own, chains[n + 1][0] + 1, True)
        else:
            m_next = scores_pass(0, 0, BLOCKS_PER_CHUNK, False)
        stats.append((m_chunk, values_pass(n, own, (a + 1) * ATTN_TQ, m_chunk, None, None)))
        m_chunk = m_next

    def body(c, carry):
        stats, m_chunk = carry
        start = pl.multiple_of(c * ATTN_CHUNK, ATTN_CHUNK)
        out = []
        for n in range(n_chains):
            if n + 1 < n_chains:
                m_next = scores_pass(n + 1, start, BLOCKS_PER_CHUNK, False)
            else:
                m_next = scores_pass(0, pl.multiple_of((c + 1) * ATTN_CHUNK, ATTN_CHUNK), BLOCKS_PER_CHUNK, False)
            m, l = stats[n]
            m_new = jnp.maximum(m, m_chunk)
            out.append((m_new, values_pass(n, start, ATTN_CHUNK, m_new, jnp.exp2(m - m_new), l)))
            m_chunk = m_next
        return tuple(out), m_chunk

    stats, _ = lax.fori_loop(0, sup, body, (tuple(stats), m_chunk))

    lam = (jnp.exp(jnp.sum(lq1_ref[...] * lk1_ref[...], axis=1, keepdims=True))
           - jnp.exp(jnp.sum(lq2_ref[...] * lk2_ref[...], axis=1, keepdims=True)) + lambda_init)
    for a in range(TILES_PER_STEP):
        out = acc_ref[2 * a] / stats[2 * a][1] - lam * (acc_ref[2 * a + 1] / stats[2 * a + 1][1])
        inv = lax.rsqrt(jnp.mean(out * out, axis=0, keepdims=True) + NORM_EPS)
        out = out * inv * subg_ref[...]
        o_ref[a * ATTN_TQ:(a + 1) * ATTN_TQ, :] = (out * (1.0 - lambda_init)).T.astype(o_ref.dtype)


def _diff_attention(qb_t, kb, vb_t, lq1, lk1, lq2, lk2, sub_g_col, batch, seq, lambda_init):
    t = batch * seq
    n_steps = seq // ATTN_CHUNK
    q_spec = lambda c: pl.BlockSpec((None, HEAD_DIM, ATTN_CHUNK), lambda b, h, i: (2 * h + c, 0, b * n_steps + i))
    k_spec = lambda c: pl.BlockSpec((None, seq, HEAD_DIM), lambda b, h, i: (2 * h + c, b, 0))
    v_spec = lambda c: pl.BlockSpec((None, HEAD_DIM, seq), lambda b, h, i: (2 * h + c, 0, b))
    vec_spec = pl.BlockSpec((1, HEAD_DIM), lambda b, h, i: (0, 0))
    return pl.pallas_call(
        functools.partial(_diff_kernel, lambda_init=lambda_init),
        grid=(batch, DIFF_HEADS, n_steps),
        in_specs=[
            q_spec(0), q_spec(1), k_spec(0), k_spec(1), v_spec(0), v_spec(1),
            vec_spec, vec_spec, vec_spec, vec_spec,
            pl.BlockSpec((DIFF_V_DIM, 1), lambda b, h, i: (0, 0)),
        ],
        out_specs=pl.BlockSpec((ATTN_CHUNK, DIFF_V_DIM), lambda b, h, i: (b * n_steps + i, h)),
        out_shape=jax.ShapeDtypeStruct((t, DIFF_HEADS * DIFF_V_DIM), BF16),
        scratch_shapes=[pltpu.VMEM((2 * TILES_PER_STEP, DIFF_V_DIM, ATTN_TQ), F32)]
                       + [pltpu.VMEM((ATTN_CHUNK, ATTN_TQ), F32)] * (2 * TILES_PER_STEP),
        compiler_params=_params(3),
        name="diff_attn",
    )(qb_t, qb_t, kb, kb, vb_t, vb_t, lq1, lk1, lq2, lk2, sub_g_col)


def _mix_kernel(oa_ref, ob_ref, sga_ref, sgb_ref, x_ref, wm_ref, wd_ref, wo_ref, g_ref, o_ref):
    ya = jnp.dot(oa_ref[...], wm_ref[...], preferred_element_type=F32)
    yb = jnp.dot(ob_ref[...], wd_ref[...], preferred_element_type=F32)
    mixed = sga_ref[...] * ya + sgb_ref[...] * yb
    z = jnp.dot(mixed.astype(BF16), wo_ref[...], preferred_element_type=F32)
    o_ref[...] = x_ref[...] + z * _rms_scale(z) * g_ref[...]


def _mix(oa, ob, gates, x2, wm, wd, wo, g, tm=256):
    t = x2.shape[0]
    const = lambda shape: pl.BlockSpec(shape, lambda i: (0, 0))
    return pl.pallas_call(
        _mix_kernel,
        grid=(t // tm,),
        in_specs=[
            pl.BlockSpec((tm, SECTION), lambda i: (i, 0)),
            pl.BlockSpec((tm, SECTION), lambda i: (i, 0)),
            pl.BlockSpec((tm, D_MODEL), lambda i: (i, 0)),
            pl.BlockSpec((tm, D_MODEL), lambda i: (i, 1)),
            pl.BlockSpec((tm, D_MODEL), lambda i: (i, 0)),
            const((SECTION, D_MODEL)), const((SECTION, D_MODEL)), const((D_MODEL, D_MODEL)),
            const((1, D_MODEL)),
        ],
        out_specs=pl.BlockSpec((tm, D_MODEL), lambda i: (i, 0)),
        out_shape=jax.ShapeDtypeStruct((t, D_MODEL), F32),
        compiler_params=_params(1),
        name="mix_out",
    )(oa, ob, gates, gates, x2, wm, wd, wo, g)


def _mlp_kernel(h_ref, gpre_ref, wup_ref, wdown_ref, gpost_ref, o_ref, u_ref, acc_ref):
    k = pl.program_id(1)

    @pl.when(k == 0)
    def _():
        hf = h_ref[...]
        u_ref[...] = (hf * _rms_scale(hf) * gpre_ref[...]).astype(BF16)
        acc_ref[...] = jnp.zeros_like(acc_ref)

    a = jnp.dot(u_ref[...], wup_ref[...], preferred_element_type=F32)
    a = jnp.square(jnp.maximum(a, 0.0)).astype(BF16)
    acc_ref[...] += jnp.dot(a, wdown_ref[...], preferred_element_type=F32)

    @pl.when(k == pl.num_programs(1) - 1)
    def _():
        ff = acc_ref[...]
        o_ref[...] = h_ref[...] + ff * _rms_scale(ff) * gpost_ref[...]


def _mlp(h, gpre, wup, wdown, gpost, tm=512, tf=1024):
    t = h.shape[0]
    return pl.pallas_call(
        _mlp_kernel,
        grid=(t // tm, D_FF // tf),
        in_specs=[
            pl.BlockSpec((tm, D_MODEL), lambda i, k: (i, 0)),
            pl.BlockSpec((1, D_MODEL), lambda i, k: (0, 0)),
            pl.BlockSpec((D_MODEL, tf), lambda i, k: (0, k)),
            pl.BlockSpec((tf, D_MODEL), lambda i, k: (k, 0)),
            pl.BlockSpec((1, D_MODEL), lambda i, k: (0, 0)),
        ],
        out_specs=pl.BlockSpec((tm, D_MODEL), lambda i, k: (i, 0)),
        out_shape=jax.ShapeDtypeStruct((t, D_MODEL), F32),
        scratch_shapes=[pltpu.VMEM((tm, D_MODEL), BF16), pltpu.VMEM((tm, D_MODEL), F32)],
        compiler_params=_params(2),
        name="mlp",
    )(h, gpre, wup, wdown, gpost)


def _ple_kernel(h_ref, p_ref, gpre_ref, wgate_ref, wproj_ref, gpost_ref, o_ref):
    hf = h_ref[...]
    u = (hf * _rms_scale(hf) * gpre_ref[...]).astype(BF16)
    gate = jax.nn.sigmoid(jnp.dot(u, wgate_ref[...], preferred_element_type=F32))
    e = jnp.dot(p_ref[...].astype(BF16), wproj_ref[...], preferred_element_type=F32) * gate
    o_ref[...] = hf + e * _rms_scale(e) * gpost_ref[...]


def _ple(h, p2, gpre, wgate, wproj, gpost, tm=512):
    t = h.shape[0]
    const = lambda shape: pl.BlockSpec(shape, lambda i: (0, 0))
    return pl.pallas_call(
        _ple_kernel,
        grid=(t // tm,),
        in_specs=[
            pl.BlockSpec((tm, D_MODEL), lambda i: (i, 0)),
            pl.BlockSpec((tm, PLE_DIM), lambda i: (i, 0)),
            const((1, D_MODEL)), const((D_MODEL, D_MODEL)), const((PLE_DIM, D_MODEL)), const((1, D_MODEL)),
        ],
        out_specs=pl.BlockSpec((tm, D_MODEL), lambda i: (i, 0)),
        out_shape=jax.ShapeDtypeStruct((t, D_MODEL), F32),
        compiler_params=_params(1),
        name="ple",
    )(h, p2, gpre, wgate, wproj, gpost)


def _rope_tables(seq):
    half = ROT_DIM // 2
    pos_f = jnp.arange(seq, dtype=F32)
    inv_freq = 1.0 / (ROPE_THETA ** (jnp.arange(half, dtype=F32) * 2.0 / ROT_DIM))
    ang = pos_f[:, None] * inv_freq[None, :]
    cos, sin = jnp.cos(ang), jnp.sin(ang)
    pad = HEAD_DIM - ROT_DIM
    zeros = jnp.zeros((seq, half), F32)
    cos_t = jnp.concatenate([cos, cos, jnp.ones((seq, pad), F32)], axis=1)
    sin_lo = jnp.concatenate([-sin, zeros, jnp.zeros((seq, pad), F32)], axis=1)
    sin_hi = jnp.concatenate([zeros, sin, jnp.zeros((seq, pad), F32)], axis=1)
    return cos_t, sin_lo, sin_hi


def kernel(x, p, w_in, w_br_moba, w_br_diff, w_out, lambda_q1, lambda_k1, lambda_q2, lambda_k2, diff_subln_g,
           g_mix_pre, g_mix_post, w_up, w_down, g_mlp_pre, g_mlp_post, w_ple_proj, w_ple_gate, g_ple_pre,
           g_ple_post):
    batch, seq, _ = x.shape
    depth = w_in.shape[0]
    t = batch * seq
    n_blocks = seq // MOBA_BLOCK
    cos_t, sin_lo, sin_hi = _rope_tables(seq)
    row = lambda v: v.reshape(1, -1).astype(F32)

    h = x.reshape(t, D_MODEL)
    for layer in range(depth):
        lambda_init = 0.8 - 0.6 * math.exp(-0.3 * layer)
        w_l = w_in[layer].astype(BF16)
        qa_t, ka, va_t, qb_t, kb, vb_t, kmean = _qkv_proj(h, row(g_mix_pre[layer]), w_l[:, :QKV_WIDTH],
                                                          cos_t, sin_lo, sin_hi, seq)
        gates = _gate_proj(h, row(g_mix_pre[layer]), w_l[:, QKV_WIDTH:])
        kmean = kmean.transpose(1, 0, 2, 3).reshape(N_HEADS, batch * n_blocks, HEAD_DIM)
        oa = _moba_attention(qa_t, ka, va_t, kmean, batch, seq)
        ob = _diff_attention(qb_t, kb, vb_t, row(lambda_q1[layer]), row(lambda_k1[layer]), row(lambda_q2[layer]),
                             row(lambda_k2[layer]), diff_subln_g[layer].reshape(-1, 1).astype(F32),
                             batch, seq, lambda_init)
        h = _mix(oa, ob, gates, h, w_br_moba[layer].astype(BF16), w_br_diff[layer].astype(BF16),
                 w_out[layer].astype(BF16), row(g_mix_post[layer]))
        h = _mlp(h, row(g_mlp_pre[layer]), w_up[layer].astype(BF16), w_down[layer].astype(BF16),
                 row(g_mlp_post[layer]))
        h = _ple(h, p[layer].reshape(t, PLE_DIM), row(g_ple_pre[layer]), w_ple_gate[layer].astype(BF16),
                 w_ple_proj[layer].astype(BF16), row(g_ple_post[layer]))
    return h.reshape(batch, seq, D_MODEL)
```

```python
import functools
import math

import jax
import jax.numpy as jnp
from jax import lax
from jax.experimental import pallas as pl
from jax.experimental.pallas import tpu as pltpu

F32 = jnp.float32
BF16 = jnp.bfloat16

D_MODEL = 2048
HEAD_DIM = 128
N_HEADS = 8
MOBA_BLOCK = 256
MOBA_TOPK = 3
DIFF_HEADS = 4
DIFF_V_DIM = 2 * HEAD_DIM
ROT_DIM = HEAD_DIM // 4
ROPE_THETA = 500000.0
D_FF = 4 * D_MODEL
PLE_DIM = 256
NORM_EPS = 1e-6
SECTION = N_HEADS * HEAD_DIM
QKV_WIDTH = 6 * SECTION
GATE_WIDTH = 2 * D_MODEL

ATTN_TQ = 256
ATTN_CHUNK = 1024
BLOCKS_PER_CHUNK = ATTN_CHUNK // MOBA_BLOCK
TILES_PER_STEP = ATTN_CHUNK // ATTN_TQ
ONES_ROWS = 16

VMEM_LIMIT_BYTES = 56 * 1024 * 1024

NEG_INF = float("-inf")
NT_DIMS = (((1,), (1,)), ((), ()))
Q_SCALE = HEAD_DIM ** -0.5 * math.log2(math.e)


def _params(n_axes):
    return pltpu.CompilerParams(dimension_semantics=("arbitrary",) * n_axes,
                                vmem_limit_bytes=VMEM_LIMIT_BYTES)


def _rms_scale(xf):
    return lax.rsqrt(jnp.mean(xf * xf, axis=-1, keepdims=True) + NORM_EPS)


def _qkv_kernel(x_ref, g_ref, w_ref, cos_ref, sin_lo_ref, sin_hi_ref,
                qa_ref, ka_ref, va_ref, qb_ref, kb_ref, vb_ref, kmean_ref, *, tm):
    xf = x_ref[...]
    u = (xf * _rms_scale(xf) * g_ref[...]).astype(BF16)
    cos, sin_lo, sin_hi = cos_ref[...], sin_lo_ref[...], sin_hi_ref[...]

    def section(j):
        y = jnp.dot(u, w_ref[:, j * SECTION:(j + 1) * SECTION], preferred_element_type=F32)
        return [y[:, h * HEAD_DIM:(h + 1) * HEAD_DIM] for h in range(N_HEADS)]

    def rope(yh):
        from_hi = pltpu.roll(yh, HEAD_DIM - ROT_DIM // 2, axis=1)
        from_lo = pltpu.roll(yh, ROT_DIM // 2, axis=1)
        return yh * cos + from_hi * sin_lo + from_lo * sin_hi

    for h, yh in enumerate(section(0)):
        qa_ref[h] = (rope(yh) * Q_SCALE).T.astype(BF16)
    for h, yh in enumerate(section(1)):
        kr = rope(yh)
        ka_ref[h] = kr.astype(BF16)
        kmean_ref[h] = jnp.mean(kr.reshape(tm // MOBA_BLOCK, MOBA_BLOCK, HEAD_DIM), axis=1)
    ones_tile = jnp.where(lax.broadcasted_iota(jnp.int32, (ONES_ROWS, tm), 0) == 0, 1.0, 0.0).astype(BF16)
    for h, yh in enumerate(section(2)):
        va_ref[h, :HEAD_DIM] = yh.T.astype(BF16)
        va_ref[h, HEAD_DIM:] = ones_tile
    for h, yh in enumerate(section(3)):
        qb_ref[h] = (rope(yh) * Q_SCALE).T.astype(BF16)
    for h, yh in enumerate(section(4)):
        kb_ref[h] = rope(yh).astype(BF16)
    for h, yh in enumerate(section(5)):
        vb_ref[h] = yh.T.astype(BF16)


def _qkv_proj(x2, g, w_qkv, cos_t, sin_lo, sin_hi, seq, tm=512):
    t = x2.shape[0]
    n_rows = t // tm
    pos_blocks = seq // tm
    row_major = jax.ShapeDtypeStruct((N_HEADS, t, HEAD_DIM), BF16)
    col_major = jax.ShapeDtypeStruct((N_HEADS, HEAD_DIM, t), BF16)
    row_spec = pl.BlockSpec((N_HEADS, tm, HEAD_DIM), lambda i: (0, i, 0))
    col_spec = pl.BlockSpec((N_HEADS, HEAD_DIM, tm), lambda i: (0, 0, i))
    tab_spec = pl.BlockSpec((tm, HEAD_DIM), lambda i: (i % pos_blocks, 0))
    return pl.pallas_call(
        functools.partial(_qkv_kernel, tm=tm),
        grid=(n_rows,),
        in_specs=[
            pl.BlockSpec((tm, D_MODEL), lambda i: (i, 0)),
            pl.BlockSpec((1, D_MODEL), lambda i: (0, 0)),
            pl.BlockSpec((D_MODEL, QKV_WIDTH), lambda i: (0, 0), pipeline_mode=pl.Buffered(1)),
            tab_spec, tab_spec, tab_spec,
        ],
        out_specs=[
            col_spec, row_spec, pl.BlockSpec((N_HEADS, HEAD_DIM + ONES_ROWS, tm), lambda i: (0, 0, i)),
            col_spec, row_spec, col_spec,
            pl.BlockSpec((None, N_HEADS, tm // MOBA_BLOCK, HEAD_DIM), lambda i: (i, 0, 0, 0)),
        ],
        out_shape=[
            col_major, row_major, jax.ShapeDtypeStruct((N_HEADS, HEAD_DIM + ONES_ROWS, t), BF16),
            col_major, row_major, col_major,
            jax.ShapeDtypeStruct((n_rows, N_HEADS, tm // MOBA_BLOCK, HEAD_DIM), F32),
        ],
        compiler_params=_params(1),
        name="qkv_proj",
    )(x2, g, w_qkv, cos_t, sin_lo, sin_hi)


def _gate_kernel(x_ref, g_ref, w_ref, o_ref, *, tn):
    xf = x_ref[...]
    u = (xf * _rms_scale(xf) * g_ref[...]).astype(BF16)
    for j in range(GATE_WIDTH // tn):
        cols = slice(j * tn, (j + 1) * tn)
        o_ref[:, cols] = jax.nn.sigmoid(jnp.dot(u, w_ref[:, cols], preferred_element_type=F32))


def _gate_proj(x2, g, w_gate, tm=512, tn=1024):
    t = x2.shape[0]
    return pl.pallas_call(
        functools.partial(_gate_kernel, tn=tn),
        grid=(t // tm,),
        in_specs=[
            pl.BlockSpec((tm, D_MODEL), lambda i: (i, 0)),
            pl.BlockSpec((1, D_MODEL), lambda i: (0, 0)),
            pl.BlockSpec((D_MODEL, GATE_WIDTH), lambda i: (0, 0), pipeline_mode=pl.Buffered(1)),
        ],
        out_specs=pl.BlockSpec((tm, GATE_WIDTH), lambda i: (i, 0)),
        out_shape=jax.ShapeDtypeStruct((t, GATE_WIDTH), F32),
        compiler_params=_params(1),
        name="gate_proj",
    )(x2, g, w_gate)


def _softmax_start(s_t):
    m = jnp.max(s_t, axis=0, keepdims=True)
    p_t = jnp.exp2(s_t - m)
    return m, jnp.sum(p_t, axis=0, keepdims=True), p_t.astype(BF16)


def _softmax_step(s_t, m, l):
    m_new = jnp.maximum(m, jnp.max(s_t, axis=0, keepdims=True))
    alpha = jnp.exp2(m - m_new)
    p_t = jnp.exp2(s_t - m_new)
    return m_new, alpha * l + jnp.sum(p_t, axis=0, keepdims=True), alpha, p_t.astype(BF16)


def _past_chunks_loop(n_chunks, body, carry):
    def pair(i, carry):
        return body(2 * i + 1, body(2 * i, carry))

    n_pairs = n_chunks // 2
    carry = lax.fori_loop(0, n_pairs, pair, carry)
    return lax.fori_loop(2 * n_pairs, n_chunks, body, carry)


def _causal_bias():
    key = lax.broadcasted_iota(jnp.int32, (ATTN_TQ, ATTN_TQ), 0)
    qry = lax.broadcasted_iota(jnp.int32, (ATTN_TQ, ATTN_TQ), 1)
    return jnp.where(key <= qry, 0.0, NEG_INF)


def _moba_kernel(q_ref, k_ref, v_ref, kmean_ref, o_ref, acc_ref, s0_ref, s1_ref, s2_ref, s3_ref, *, n_blocks):
    sup = pl.program_id(2)
    q_all = q_ref[...]

    km = kmean_ref[...]
    km_hi = km.astype(BF16)
    km_mid = (km - km_hi.astype(F32)).astype(BF16)
    km_lo = (km - km_hi.astype(F32) - km_mid.astype(F32)).astype(BF16)
    gate = (jnp.dot(km_lo, q_all, preferred_element_type=F32) + jnp.dot(km_mid, q_all, preferred_element_type=F32)
            + jnp.dot(km_hi, q_all, preferred_element_type=F32))
    blk = lax.broadcasted_iota(jnp.int32, gate.shape, 0).astype(F32)
    own_blk = (sup * TILES_PER_STEP
               + lax.broadcasted_iota(jnp.int32, (1, ATTN_CHUNK), 1) // MOBA_BLOCK).astype(F32)
    g = jnp.where(blk < own_blk, gate, NEG_INF)
    picks = []
    for _ in range(MOBA_TOPK):
        best = jnp.max(g, axis=0, keepdims=True)
        first = jnp.min(jnp.where(g == best, blk, float(n_blocks)), axis=0, keepdims=True)
        first = jnp.where(best > NEG_INF, first, -1.0)
        picks.append(first)
        g = jnp.where(blk == first, NEG_INF, g)

    def lanes(a):
        return slice(a * ATTN_TQ, (a + 1) * ATTN_TQ)

    q_t = [q_all[:, lanes(a)] for a in range(TILES_PER_STEP)]
    tile_picks = [[p[:, lanes(a)] for p in picks] for a in range(TILES_PER_STEP)]
    s_refs = (s0_ref, s1_ref, s2_ref, s3_ref)
    causal_bias = _causal_bias()

    def selection_bias(a, block_index):
        jf = jnp.asarray(block_index).astype(F32)
        chosen = (tile_picks[a][0] == jf) | (tile_picks[a][1] == jf) | (tile_picks[a][2] == jf)
        return jnp.where(chosen, 0.0, NEG_INF)

    def past_biases(a, c):
        return [selection_bias(a, c * BLOCKS_PER_CHUNK + r) for r in range(BLOCKS_PER_CHUNK)]

    def own_biases(a):
        return [selection_bias(a, sup * TILES_PER_STEP + r) for r in range(a)] + [causal_bias]

    def scores_pass(a, start, biases):
        top = None
        for r, bias in enumerate(biases):
            k_r = k_ref[pl.ds(start + r * MOBA_BLOCK, MOBA_BLOCK), :]
            s_r = jnp.dot(k_r, q_t[a], preferred_element_type=F32) + bias
            s_refs[a][r * MOBA_BLOCK:(r + 1) * MOBA_BLOCK, :] = s_r
            part = jnp.max(s_r.reshape(MOBA_BLOCK // 8, 8, ATTN_TQ), axis=0)
            top = part if top is None else jnp.maximum(top, part)
        return jnp.max(top, axis=0, keepdims=True)

    def values_pass(a, start, rows, m_new, alpha):
        p_t = jnp.exp2(s_refs[a][:rows, :] - m_new).astype(BF16)
        update = jnp.dot(v_ref[:, pl.ds(start, rows)], p_t, preferred_element_type=F32)
        acc_ref[a] = update if alpha is None else alpha * acc_ref[a] + update

    own = pl.multiple_of(sup * ATTN_CHUNK, ATTN_CHUNK)
    m_chunk = scores_pass(0, own, own_biases(0))
    tops = []
    for a in range(TILES_PER_STEP):
        if a + 1 < TILES_PER_STEP:
            m_next = scores_pass(a + 1, own, own_biases(a + 1))
        else:
            m_next = scores_pass(0, 0, past_biases(0, 0))
        values_pass(a, own, (a + 1) * MOBA_BLOCK, m_chunk, None)
        tops.append(m_chunk)
        m_chunk = m_next

    def body(c, carry):
        tops, m_chunk = carry
        start = pl.multiple_of(c * ATTN_CHUNK, ATTN_CHUNK)
        out = []
        for a in range(TILES_PER_STEP):
            if a + 1 < TILES_PER_STEP:
                m_next = scores_pass(a + 1, start, past_biases(a + 1, c))
            else:
                m_next = scores_pass(0, pl.multiple_of((c + 1) * ATTN_CHUNK, ATTN_CHUNK), past_biases(0, c + 1))
            m_new = jnp.maximum(tops[a], m_chunk)
            values_pass(a, start, ATTN_CHUNK, m_new, jnp.exp2(tops[a] - m_new))
            out.append(m_new)
            m_chunk = m_next
        return tuple(out), m_chunk

    _past_chunks_loop(sup, body, (tuple(tops), m_chunk))
    for a in range(TILES_PER_STEP):
        acc = acc_ref[a]
        o_ref[lanes(a), :] = (acc[:HEAD_DIM] / acc[HEAD_DIM:HEAD_DIM + 1]).T.astype(o_ref.dtype)


def _moba_attention(qa_t, ka, va_t, kmean, batch, seq):
    t = batch * seq
    n_blocks = seq // MOBA_BLOCK
    n_steps = seq // ATTN_CHUNK
    return pl.pallas_call(
        functools.partial(_moba_kernel, n_blocks=n_blocks),
        grid=(batch, N_HEADS, n_steps),
        in_specs=[
            pl.BlockSpec((None, HEAD_DIM, ATTN_CHUNK), lambda b, h, i: (h, 0, b * n_steps + i)),
            pl.BlockSpec((None, seq, HEAD_DIM), lambda b, h, i: (h, b, 0)),
            pl.BlockSpec((None, HEAD_DIM + ONES_ROWS, seq), lambda b, h, i: (h, 0, b)),
            pl.BlockSpec((None, n_blocks, HEAD_DIM), lambda b, h, i: (h, b, 0)),
        ],
        out_specs=pl.BlockSpec((ATTN_CHUNK, HEAD_DIM), lambda b, h, i: (b * n_steps + i, h)),
        out_shape=jax.ShapeDtypeStruct((t, SECTION), BF16),
        scratch_shapes=[pltpu.VMEM((TILES_PER_STEP, HEAD_DIM + ONES_ROWS, ATTN_TQ), F32)]
                       + [pltpu.VMEM((ATTN_CHUNK, ATTN_TQ), F32)] * TILES_PER_STEP,
        compiler_params=_params(3),
        name="moba_attn",
    )(qa_t, ka, va_t, kmean)


def _diff_kernel(q0_ref, q1_ref, k0_ref, k1_ref, vlo_ref, vhi_ref, lq1_ref, lk1_ref, lq2_ref, lk2_ref, subg_ref,
                 o_ref, acc_ref, *s_refs, lambda_init):
    sup = pl.program_id(2)
    q_refs = (q0_ref, q1_ref)
    k_refs = (k0_ref, k1_ref)
    chains = [(a, sub) for a in range(TILES_PER_STEP) for sub in range(2)]
    n_chains = len(chains)
    q_t = [q_refs[sub][:, a * ATTN_TQ:(a + 1) * ATTN_TQ] for a, sub in chains]
    causal_bias = _causal_bias()

    def scores_pass(n, start, n_blocks, causal_last):
        top = None
        for r in range(n_blocks):
            k_r = k_refs[chains[n][1]][pl.ds(start + r * ATTN_TQ, ATTN_TQ), :]
            s_r = jnp.dot(k_r, q_t[n], preferred_element_type=F32)
            if causal_last and r == n_blocks - 1:
                s_r = s_r + causal_bias
            s_refs[n][r * ATTN_TQ:(r + 1) * ATTN_TQ, :] = s_r
            part = jnp.max(s_r.reshape(ATTN_TQ // 8, 8, ATTN_TQ), axis=0)
            top = part if top is None else jnp.maximum(top, part)
        return jnp.max(top, axis=0, keepdims=True)

    def values_pass(n, start, rows, m_new, alpha, l):
        p_t = jnp.exp2(s_refs[n][:rows, :] - m_new)
        l_chunk = jnp.sum(jnp.sum(p_t.reshape(rows // 8, 8, ATTN_TQ), axis=0), axis=0, keepdims=True)
        v_t = jnp.concatenate([vlo_ref[:, pl.ds(start, rows)], vhi_ref[:, pl.ds(start, rows)]], axis=0)
        update = jnp.dot(v_t, p_t.astype(BF16), preferred_element_type=F32)
        if alpha is None:
            acc_ref[n] = update
            return l_chunk
        acc_ref[n] = alpha * acc_ref[n] + update
        return alpha * l + l_chunk

    own = pl.multiple_of(sup * ATTN_CHUNK, ATTN_CHUNK)
    m_chunk = scores_pass(0, own, 1, True)
    stats = []
    for n, (a, sub) in enumerate(chains):
        if n + 1 < n_chains:
            m_next = scores_pass(n + 1, own, chains[n + 1][0] + 1, True)
        else:
            m_next = scores_pass(0, 0, BLOCKS_PER_CHUNK, False)
        stats.append((m_chunk, values_pass(n, own, (a + 1) * ATTN_TQ, m_chunk, None, None)))
        m_chunk = m_next

    def body(c, carry):
        stats, m_chunk = carry
        start = pl.multiple_of(c * ATTN_CHUNK, ATTN_CHUNK)
        out = []
        for n in range(n_chains):
            if n + 1 < n_chains:
                m_next = scores_pass(n + 1, start, BLOCKS_PER_CHUNK, False)
            else:
                m_next = scores_pass(0, pl.multiple_of((c + 1) * ATTN_CHUNK, ATTN_CHUNK), BLOCKS_PER_CHUNK, False)
            m, l = stats[n]
            m_new = jnp.maximum(m, m_chunk)
            out.append((m_new, values_pass(n, start, ATTN_CHUNK, m_new, jnp.exp2(m - m_new), l)))
            m_chunk = m_next
        return tuple(out), m_chunk

    stats, _ = _past_chunks_loop(sup, body, (tuple(stats), m_chunk))

    lam = (jnp.exp(jnp.sum(lq1_ref[...] * lk1_ref[...], axis=1, keepdims=True))
           - jnp.exp(jnp.sum(lq2_ref[...] * lk2_ref[...], axis=1, keepdims=True)) + lambda_init)
    for a in range(TILES_PER_STEP):
        out = acc_ref[2 * a] / stats[2 * a][1] - lam * (acc_ref[2 * a + 1] / stats[2 * a + 1][1])
        inv = lax.rsqrt(jnp.mean(out * out, axis=0, keepdims=True) + NORM_EPS)
        out = out * inv * subg_ref[...]
        o_ref[a * ATTN_TQ:(a + 1) * ATTN_TQ, :] = (out * (1.0 - lambda_init)).T.astype(o_ref.dtype)


def _diff_attention(qb_t, kb, vb_t, lq1, lk1, lq2, lk2, sub_g_col, batch, seq, lambda_init):
    t = batch * seq
    n_steps = seq // ATTN_CHUNK
    q_spec = lambda c: pl.BlockSpec((None, HEAD_DIM, ATTN_CHUNK), lambda b, h, i: (2 * h + c, 0, b * n_steps + i))
    k_spec = lambda c: pl.BlockSpec((None, seq, HEAD_DIM), lambda b, h, i: (2 * h + c, b, 0))
    v_spec = lambda c: pl.BlockSpec((None, HEAD_DIM, seq), lambda b, h, i: (2 * h + c, 0, b))
    vec_spec = pl.BlockSpec((1, HEAD_DIM), lambda b, h, i: (0, 0))
    return pl.pallas_call(
        functools.partial(_diff_kernel, lambda_init=lambda_init),
        grid=(batch, DIFF_HEADS, n_steps),
        in_specs=[
            q_spec(0), q_spec(1), k_spec(0), k_spec(1), v_spec(0), v_spec(1),
            vec_spec, vec_spec, vec_spec, vec_spec,
            pl.BlockSpec((DIFF_V_DIM, 1), lambda b, h, i: (0, 0)),
        ],
        out_specs=pl.BlockSpec((ATTN_CHUNK, DIFF_V_DIM), lambda b, h, i: (b * n_steps + i, h)),
        out_shape=jax.ShapeDtypeStruct((t, DIFF_HEADS * DIFF_V_DIM), BF16),
        scratch_shapes=[pltpu.VMEM((2 * TILES_PER_STEP, DIFF_V_DIM, ATTN_TQ), F32)]
                       + [pltpu.VMEM((ATTN_CHUNK, ATTN_TQ), F32)] * (2 * TILES_PER_STEP),
        compiler_params=_params(3),
        name="diff_attn",
    )(qb_t, qb_t, kb, kb, vb_t, vb_t, lq1, lk1, lq2, lk2, sub_g_col)


def _mix_kernel(oa_ref, ob_ref, sga_ref, sgb_ref, x_ref, wm_ref, wd_ref, wo_ref, g_ref, o_ref):
    ya = jnp.dot(oa_ref[...], wm_ref[...], preferred_element_type=F32)
    yb = jnp.dot(ob_ref[...], wd_ref[...], preferred_element_type=F32)
    mixed = sga_ref[...] * ya + sgb_ref[...] * yb
    z = jnp.dot(mixed.astype(BF16), wo_ref[...], preferred_element_type=F32)
    o_ref[...] = x_ref[...] + z * _rms_scale(z) * g_ref[...]


def _mix(oa, ob, gates, x2, wm, wd, wo, g, tm=256):
    t = x2.shape[0]
    const = lambda shape: pl.BlockSpec(shape, lambda i: (0, 0))
    return pl.pallas_call(
        _mix_kernel,
        grid=(t // tm,),
        in_specs=[
            pl.BlockSpec((tm, SECTION), lambda i: (i, 0)),
            pl.BlockSpec((tm, SECTION), lambda i: (i, 0)),
            pl.BlockSpec((tm, D_MODEL), lambda i: (i, 0)),
            pl.BlockSpec((tm, D_MODEL), lambda i: (i, 1)),
            pl.BlockSpec((tm, D_MODEL), lambda i: (i, 0)),
            const((SECTION, D_MODEL)), const((SECTION, D_MODEL)), const((D_MODEL, D_MODEL)),
            const((1, D_MODEL)),
        ],
        out_specs=pl.BlockSpec((tm, D_MODEL), lambda i: (i, 0)),
        out_shape=jax.ShapeDtypeStruct((t, D_MODEL), F32),
        compiler_params=_params(1),
        name="mix_out",
    )(oa, ob, gates, gates, x2, wm, wd, wo, g)


def _mlp_kernel(h_ref, gpre_ref, wup_ref, wdown_ref, gpost_ref, o_ref, u_ref, acc_ref):
    k = pl.program_id(1)

    @pl.when(k == 0)
    def _():
        hf = h_ref[...]
        u_ref[...] = (hf * _rms_scale(hf) * gpre_ref[...]).astype(BF16)
        acc_ref[...] = jnp.zeros_like(acc_ref)

    a = jnp.dot(u_ref[...], wup_ref[...], preferred_element_type=F32)
    a = jnp.square(jnp.maximum(a, 0.0)).astype(BF16)
    acc_ref[...] += jnp.dot(a, wdown_ref[...], preferred_element_type=F32)

    @pl.when(k == pl.num_programs(1) - 1)
    def _():
        ff = acc_ref[...]
        o_ref[...] = h_ref[...] + ff * _rms_scale(ff) * gpost_ref[...]


def _mlp(h, gpre, wup, wdown, gpost, tm=512, tf=1024):
    t = h.shape[0]
    return pl.pallas_call(
        _mlp_kernel,
        grid=(t // tm, D_FF // tf),
        in_specs=[
            pl.BlockSpec((tm, D_MODEL), lambda i, k: (i, 0)),
            pl.BlockSpec((1, D_MODEL), lambda i, k: (0, 0)),
            pl.BlockSpec((D_MODEL, tf), lambda i, k: (0, k)),
            pl.BlockSpec((tf, D_MODEL), lambda i, k: (k, 0)),
            pl.BlockSpec((1, D_MODEL), lambda i, k: (0, 0)),
        ],
        out_specs=pl.BlockSpec((tm, D_MODEL), lambda i, k: (i, 0)),
        out_shape=jax.ShapeDtypeStruct((t, D_MODEL), F32),
        scratch_shapes=[pltpu.VMEM((tm, D_MODEL), BF16), pltpu.VMEM((tm, D_MODEL), F32)],
        compiler_params=_params(2),
        name="mlp",
    )(h, gpre, wup, wdown, gpost)


def _ple_kernel(h_ref, p_ref, gpre_ref, wgate_ref, wproj_ref, gpost_ref, o_ref):
    hf = h_ref[...]
    u = (hf * _rms_scale(hf) * gpre_ref[...]).astype(BF16)
    gate = jax.nn.sigmoid(jnp.dot(u, wgate_ref[...], preferred_element_type=F32))
    e = jnp.dot(p_ref[...].astype(BF16), wproj_ref[...], preferred_element_type=F32) * gate
    o_ref[...] = hf + e * _rms_scale(e) * gpost_ref[...]


def _ple(h, p2, gpre, wgate, wproj, gpost, tm=512):
    t = h.shape[0]
    const = lambda shape: pl.BlockSpec(shape, lambda i: (0, 0))
    return pl.pallas_call(
        _ple_kernel,
        grid=(t // tm,),
        in_specs=[
            pl.BlockSpec((tm, D_MODEL), lambda i: (i, 0)),
            pl.BlockSpec((tm, PLE_DIM), lambda i: (i, 0)),
            const((1, D_MODEL)), const((D_MODEL, D_MODEL)), const((PLE_DIM, D_MODEL)), const((1, D_MODEL)),
        ],
        out_specs=pl.BlockSpec((tm, D_MODEL), lambda i: (i, 0)),
        out_shape=jax.ShapeDtypeStruct((t, D_MODEL), F32),
        compiler_params=_params(1),
        name="ple",
    )(h, p2, gpre, wgate, wproj, gpost)


def _rope_tables(seq):
    half = ROT_DIM // 2
    pos_f = jnp.arange(seq, dtype=F32)
    inv_freq = 1.0 / (ROPE_THETA ** (jnp.arange(half, dtype=F32) * 2.0 / ROT_DIM))
    ang = pos_f[:, None] * inv_freq[None, :]
    cos, sin = jnp.cos(ang), jnp.sin(ang)
    pad = HEAD_DIM - ROT_DIM
    zeros = jnp.zeros((seq, half), F32)
    cos_t = jnp.concatenate([cos, cos, jnp.ones((seq, pad), F32)], axis=1)
    sin_lo = jnp.concatenate([-sin, zeros, jnp.zeros((seq, pad), F32)], axis=1)
    sin_hi = jnp.concatenate([zeros, sin, jnp.zeros((seq, pad), F32)], axis=1)
    return cos_t, sin_lo, sin_hi


def kernel(x, p, w_in, w_br_moba, w_br_diff, w_out, lambda_q1, lambda_k1, lambda_q2, lambda_k2, diff_subln_g,
           g_mix_pre, g_mix_post, w_up, w_down, g_mlp_pre, g_mlp_post, w_ple_proj, w_ple_gate, g_ple_pre,
           g_ple_post):
    batch, seq, _ = x.shape
    depth = w_in.shape[0]
    t = batch * seq
    n_blocks = seq // MOBA_BLOCK
    cos_t, sin_lo, sin_hi = _rope_tables(seq)
    row = lambda v: v.reshape(1, -1).astype(F32)

    h = x.reshape(t, D_MODEL)
    for layer in range(depth):
        lambda_init = 0.8 - 0.6 * math.exp(-0.3 * layer)
        w_l = w_in[layer].astype(BF16)
        qa_t, ka, va_t, qb_t, kb, vb_t, kmean = _qkv_proj(h, row(g_mix_pre[layer]), w_l[:, :QKV_WIDTH],
                                                          cos_t, sin_lo, sin_hi, seq)
        gates = _gate_proj(h, row(g_mix_pre[layer]), w_l[:, QKV_WIDTH:])
        kmean = kmean.transpose(1, 0, 2, 3).reshape(N_HEADS, batch * n_blocks, HEAD_DIM)
        oa = _moba_attention(qa_t, ka, va_t, kmean, batch, seq)
        ob = _diff_attention(qb_t, kb, vb_t, row(lambda_q1[layer]), row(lambda_k1[layer]), row(lambda_q2[layer]),
                             row(lambda_k2[layer]), diff_subln_g[layer].reshape(-1, 1).astype(F32),
                             batch, seq, lambda_init)
        h = _mix(oa, ob, gates, h, w_br_moba[layer].astype(BF16), w_br_diff[layer].astype(BF16),
                 w_out[layer].astype(BF16), row(g_mix_post[layer]))
        h = _mlp(h, row(g_mlp_pre[layer]), w_up[layer].astype(BF16), w_down[layer].astype(BF16),
                 row(g_mlp_post[layer]))
        h = _ple(h, p[layer].reshape(t, PLE_DIM), row(g_ple_pre[layer]), w_ple_gate[layer].astype(BF16),
                 w_ple_proj[layer].astype(BF16), row(g_ple_post[layer]))
    return h.reshape(batch, seq, D_MODEL)
```

```python
import functools
import math

import jax
import jax.numpy as jnp
from jax import lax
from jax.experimental import pallas as pl
from jax.experimental.pallas import tpu as pltpu

F32 = jnp.float32
BF16 = jnp.bfloat16

D_MODEL = 2048
HEAD_DIM = 128
N_HEADS = 8
MOBA_BLOCK = 256
MOBA_TOPK = 3
DIFF_HEADS = 4
DIFF_V_DIM = 2 * HEAD_DIM
ROT_DIM = HEAD_DIM // 4
ROPE_THETA = 500000.0
D_FF = 4 * D_MODEL
PLE_DIM = 256
NORM_EPS = 1e-6
SECTION = N_HEADS * HEAD_DIM
QKV_WIDTH = 6 * SECTION
GATE_WIDTH = 2 * D_MODEL

ATTN_TQ = 256
ATTN_CHUNK = 1024
BLOCKS_PER_CHUNK = ATTN_CHUNK // MOBA_BLOCK
TILES_PER_STEP = ATTN_CHUNK // ATTN_TQ
ONES_ROWS = 16
LOOP_UNROLLS = (4, 2, 1)

VMEM_LIMIT_BYTES = 56 * 1024 * 1024

NEG_INF = float("-inf")
NT_DIMS = (((1,), (1,)), ((), ()))
Q_SCALE = HEAD_DIM ** -0.5 * math.log2(math.e)


def _params(n_axes):
    return pltpu.CompilerParams(dimension_semantics=("arbitrary",) * n_axes,
                                vmem_limit_bytes=VMEM_LIMIT_BYTES)


def _rms_scale(xf):
    return lax.rsqrt(jnp.mean(xf * xf, axis=-1, keepdims=True) + NORM_EPS)


def _qkv_kernel(x_ref, g_ref, w_ref, cos_ref, sin_lo_ref, sin_hi_ref,
                qa_ref, ka_ref, va_ref, qb_ref, kb_ref, vb_ref, kmean_ref, *, tm):
    xf = x_ref[...]
    u = (xf * _rms_scale(xf) * g_ref[...]).astype(BF16)
    cos, sin_lo, sin_hi = cos_ref[...], sin_lo_ref[...], sin_hi_ref[...]

    def section(j):
        y = jnp.dot(u, w_ref[:, j * SECTION:(j + 1) * SECTION], preferred_element_type=F32)
        return [y[:, h * HEAD_DIM:(h + 1) * HEAD_DIM] for h in range(N_HEADS)]

    def rope(yh):
        from_hi = pltpu.roll(yh, HEAD_DIM - ROT_DIM // 2, axis=1)
        from_lo = pltpu.roll(yh, ROT_DIM // 2, axis=1)
        return yh * cos + from_hi * sin_lo + from_lo * sin_hi

    for h, yh in enumerate(section(0)):
        qa_ref[h] = (rope(yh) * Q_SCALE).T.astype(BF16)
    for h, yh in enumerate(section(1)):
        kr = rope(yh)
        ka_ref[h] = kr.astype(BF16)
        kmean_ref[h] = jnp.mean(kr.reshape(tm // MOBA_BLOCK, MOBA_BLOCK, HEAD_DIM), axis=1)
    ones_tile = jnp.where(lax.broadcasted_iota(jnp.int32, (ONES_ROWS, tm), 0) == 0, 1.0, 0.0).astype(BF16)
    for h, yh in enumerate(section(2)):
        va_ref[h, :HEAD_DIM] = yh.T.astype(BF16)
        va_ref[h, HEAD_DIM:] = ones_tile
    for h, yh in enumerate(section(3)):
        qb_ref[h] = (rope(yh) * Q_SCALE).T.astype(BF16)
    for h, yh in enumerate(section(4)):
        kb_ref[h] = rope(yh).astype(BF16)
    for h, yh in enumerate(section(5)):
        vb_ref[h] = yh.T.astype(BF16)


def _qkv_proj(x2, g, w_qkv, cos_t, sin_lo, sin_hi, seq, tm=512):
    t = x2.shape[0]
    n_rows = t // tm
    pos_blocks = seq // tm
    row_major = jax.ShapeDtypeStruct((N_HEADS, t, HEAD_DIM), BF16)
    col_major = jax.ShapeDtypeStruct((N_HEADS, HEAD_DIM, t), BF16)
    row_spec = pl.BlockSpec((N_HEADS, tm, HEAD_DIM), lambda i: (0, i, 0))
    col_spec = pl.BlockSpec((N_HEADS, HEAD_DIM, tm), lambda i: (0, 0, i))
    tab_spec = pl.BlockSpec((tm, HEAD_DIM), lambda i: (i % pos_blocks, 0))
    return pl.pallas_call(
        functools.partial(_qkv_kernel, tm=tm),
        grid=(n_rows,),
        in_specs=[
            pl.BlockSpec((tm, D_MODEL), lambda i: (i, 0)),
            pl.BlockSpec((1, D_MODEL), lambda i: (0, 0)),
            pl.BlockSpec((D_MODEL, QKV_WIDTH), lambda i: (0, 0), pipeline_mode=pl.Buffered(1)),
            tab_spec, tab_spec, tab_spec,
        ],
        out_specs=[
            col_spec, row_spec, pl.BlockSpec((N_HEADS, HEAD_DIM + ONES_ROWS, tm), lambda i: (0, 0, i)),
            col_spec, row_spec, col_spec,
            pl.BlockSpec((None, N_HEADS, tm // MOBA_BLOCK, HEAD_DIM), lambda i: (i, 0, 0, 0)),
        ],
        out_shape=[
            col_major, row_major, jax.ShapeDtypeStruct((N_HEADS, HEAD_DIM + ONES_ROWS, t), BF16),
            col_major, row_major, col_major,
            jax.ShapeDtypeStruct((n_rows, N_HEADS, tm // MOBA_BLOCK, HEAD_DIM), F32),
        ],
        compiler_params=_params(1),
        name="qkv_proj",
    )(x2, g, w_qkv, cos_t, sin_lo, sin_hi)


def _gate_kernel(x_ref, g_ref, w_ref, o_ref, *, tn):
    xf = x_ref[...]
    u = (xf * _rms_scale(xf) * g_ref[...]).astype(BF16)
    for j in range(GATE_WIDTH // tn):
        cols = slice(j * tn, (j + 1) * tn)
        o_ref[:, cols] = jax.nn.sigmoid(jnp.dot(u, w_ref[:, cols], preferred_element_type=F32))


def _gate_proj(x2, g, w_gate, tm=512, tn=1024):
    t = x2.shape[0]
    return pl.pallas_call(
        functools.partial(_gate_kernel, tn=tn),
        grid=(t // tm,),
        in_specs=[
            pl.BlockSpec((tm, D_MODEL), lambda i: (i, 0)),
            pl.BlockSpec((1, D_MODEL), lambda i: (0, 0)),
            pl.BlockSpec((D_MODEL, GATE_WIDTH), lambda i: (0, 0), pipeline_mode=pl.Buffered(1)),
        ],
        out_specs=pl.BlockSpec((tm, GATE_WIDTH), lambda i: (i, 0)),
        out_shape=jax.ShapeDtypeStruct((t, GATE_WIDTH), F32),
        compiler_params=_params(1),
        name="gate_proj",
    )(x2, g, w_gate)


def _past_chunks_loop(n_chunks, body, carry):
    done = 0
    for width in LOOP_UNROLLS:
        def group(i, carry, width=width, base=done):
            for u in range(width):
                carry = body(base + width * i + u, carry)
            return carry

        trips = (n_chunks - done) // width
        carry = lax.fori_loop(0, trips, group, carry)
        done = done + trips * width
    return carry


def _causal_bias():
    key = lax.broadcasted_iota(jnp.int32, (ATTN_TQ, ATTN_TQ), 0)
    qry = lax.broadcasted_iota(jnp.int32, (ATTN_TQ, ATTN_TQ), 1)
    return jnp.where(key <= qry, 0.0, NEG_INF)


def _moba_kernel(q_ref, k_ref, v_ref, kmean_ref, o_ref, acc_ref, s0_ref, s1_ref, s2_ref, s3_ref, *, n_blocks):
    sup = pl.program_id(2)
    q_all = q_ref[...]

    km = kmean_ref[...]
    km_hi = km.astype(BF16)
    km_mid = (km - km_hi.astype(F32)).astype(BF16)
    km_lo = (km - km_hi.astype(F32) - km_mid.astype(F32)).astype(BF16)
    gate = (jnp.dot(km_lo, q_all, preferred_element_type=F32) + jnp.dot(km_mid, q_all, preferred_element_type=F32)
            + jnp.dot(km_hi, q_all, preferred_element_type=F32))
    blk = lax.broadcasted_iota(jnp.int32, gate.shape, 0).astype(F32)
    own_blk = (sup * TILES_PER_STEP
               + lax.broadcasted_iota(jnp.int32, (1, ATTN_CHUNK), 1) // MOBA_BLOCK).astype(F32)
    g = jnp.where(blk < own_blk, gate, NEG_INF)
    picks = []
    for _ in range(MOBA_TOPK):
        best = jnp.max(g, axis=0, keepdims=True)
        first = jnp.min(jnp.where(g == best, blk, float(n_blocks)), axis=0, keepdims=True)
        first = jnp.where(best > NEG_INF, first, -1.0)
        picks.append(first)
        g = jnp.where(blk == first, NEG_INF, g)

    def lanes(a):
        return slice(a * ATTN_TQ, (a + 1) * ATTN_TQ)

    q_t = [q_all[:, lanes(a)] for a in range(TILES_PER_STEP)]
    tile_picks = [[p[:, lanes(a)] for p in picks] for a in range(TILES_PER_STEP)]
    s_refs = (s0_ref, s1_ref, s2_ref, s3_ref)
    causal_bias = _causal_bias()

    def selection_bias(a, block_index):
        jf = jnp.asarray(block_index).astype(F32)
        chosen = (tile_picks[a][0] == jf) | (tile_picks[a][1] == jf) | (tile_picks[a][2] == jf)
        return jnp.where(chosen, 0.0, NEG_INF)

    def past_biases(a, c):
        return [selection_bias(a, c * BLOCKS_PER_CHUNK + r) for r in range(BLOCKS_PER_CHUNK)]

    def own_biases(a):
        return [selection_bias(a, sup * TILES_PER_STEP + r) for r in range(a)] + [causal_bias]

    def scores_pass(a, start, biases):
        top = None
        for r, bias in enumerate(biases):
            k_r = k_ref[pl.ds(start + r * MOBA_BLOCK, MOBA_BLOCK), :]
            s_r = jnp.dot(k_r, q_t[a], preferred_element_type=F32)
            if bias.shape[0] != 1:
                s_r = s_r + bias
            s_refs[a][r * MOBA_BLOCK:(r + 1) * MOBA_BLOCK, :] = s_r
            part = jnp.max(s_r.reshape(MOBA_BLOCK // 8, 8, ATTN_TQ), axis=0)
            if bias.shape[0] == 1:
                part = part + bias
            top = part if top is None else jnp.maximum(top, part)
        return jnp.max(top, axis=0, keepdims=True)

    def values_pass(a, start, biases, m_new, alpha):
        p_blocks = []
        for r, bias in enumerate(biases):
            shift = m_new - bias if bias.shape[0] == 1 else m_new
            p_blocks.append(jnp.exp2(s_refs[a][r * MOBA_BLOCK:(r + 1) * MOBA_BLOCK, :] - shift).astype(BF16))
        rows = len(biases) * MOBA_BLOCK
        update = jnp.dot(v_ref[:, pl.ds(start, rows)], jnp.concatenate(p_blocks, axis=0),
                         preferred_element_type=F32)
        acc_ref[a] = update if alpha is None else alpha * acc_ref[a] + update

    own = pl.multiple_of(sup * ATTN_CHUNK, ATTN_CHUNK)
    m_chunk = scores_pass(0, own, own_biases(0))
    tops = []
    for a in range(TILES_PER_STEP):
        if a + 1 < TILES_PER_STEP:
            m_next = scores_pass(a + 1, own, own_biases(a + 1))
        else:
            m_next = scores_pass(0, 0, past_biases(0, 0))
        values_pass(a, own, own_biases(a), m_chunk, None)
        tops.append(m_chunk)
        m_chunk = m_next

    def body(c, carry):
        tops, m_chunk = carry
        start = pl.multiple_of(c * ATTN_CHUNK, ATTN_CHUNK)
        out = []
        for a in range(TILES_PER_STEP):
            if a + 1 < TILES_PER_STEP:
                m_next = scores_pass(a + 1, start, past_biases(a + 1, c))
            else:
                m_next = scores_pass(0, pl.multiple_of((c + 1) * ATTN_CHUNK, ATTN_CHUNK), past_biases(0, c + 1))
            m_new = jnp.maximum(tops[a], m_chunk)
            values_pass(a, start, past_biases(a, c), m_new, jnp.exp2(tops[a] - m_new))
            out.append(m_new)
            m_chunk = m_next
        return tuple(out), m_chunk

    _past_chunks_loop(sup, body, (tuple(tops), m_chunk))
    for a in range(TILES_PER_STEP):
        acc = acc_ref[a]
        o_ref[lanes(a), :] = (acc[:HEAD_DIM] / acc[HEAD_DIM:HEAD_DIM + 1]).T.astype(o_ref.dtype)


def _moba_attention(qa_t, ka, va_t, kmean, batch, seq):
    t = batch * seq
    n_blocks = seq // MOBA_BLOCK
    n_steps = seq // ATTN_CHUNK
    return pl.pallas_call(
        functools.partial(_moba_kernel, n_blocks=n_blocks),
        grid=(batch, N_HEADS, n_steps),
        in_specs=[
            pl.BlockSpec((None, HEAD_DIM, ATTN_CHUNK), lambda b, h, i: (h, 0, b * n_steps + i)),
            pl.BlockSpec((None, seq, HEAD_DIM), lambda b, h, i: (h, b, 0)),
            pl.BlockSpec((None, HEAD_DIM + ONES_ROWS, seq), lambda b, h, i: (h, 0, b)),
            pl.BlockSpec((None, n_blocks, HEAD_DIM), lambda b, h, i: (h, b, 0)),
        ],
        out_specs=pl.BlockSpec((ATTN_CHUNK, HEAD_DIM), lambda b, h, i: (b * n_steps + i, h)),
        out_shape=jax.ShapeDtypeStruct((t, SECTION), BF16),
        scratch_shapes=[pltpu.VMEM((TILES_PER_STEP, HEAD_DIM + ONES_ROWS, ATTN_TQ), F32)]
                       + [pltpu.VMEM((ATTN_CHUNK, ATTN_TQ), F32)] * TILES_PER_STEP,
        compiler_params=_params(3),
        name="moba_attn",
    )(qa_t, ka, va_t, kmean)


def _diff_kernel(q0_ref, q1_ref, k0_ref, k1_ref, vlo_ref, vhi_ref, lq1_ref, lk1_ref, lq2_ref, lk2_ref, subg_ref,
                 o_ref, acc_ref, *s_refs, lambda_init):
    sup = pl.program_id(2)
    q_refs = (q0_ref, q1_ref)
    k_refs = (k0_ref, k1_ref)
    chains = [(a, sub) for a in range(TILES_PER_STEP) for sub in range(2)]
    n_chains = len(chains)
    q_t = [q_refs[sub][:, a * ATTN_TQ:(a + 1) * ATTN_TQ] for a, sub in chains]
    causal_bias = _causal_bias()

    def scores_pass(n, start, n_blocks, causal_last):
        top = None
        for r in range(n_blocks):
            k_r = k_refs[chains[n][1]][pl.ds(start + r * ATTN_TQ, ATTN_TQ), :]
            s_r = jnp.dot(k_r, q_t[n], preferred_element_type=F32)
            if causal_last and r == n_blocks - 1:
                s_r = s_r + causal_bias
            s_refs[n][r * ATTN_TQ:(r + 1) * ATTN_TQ, :] = s_r
            part = jnp.max(s_r.reshape(ATTN_TQ // 8, 8, ATTN_TQ), axis=0)
            top = part if top is None else jnp.maximum(top, part)
        return jnp.max(top, axis=0, keepdims=True)

    def values_pass(n, start, rows, m_new, alpha, l):
        p_t = jnp.exp2(s_refs[n][:rows, :] - m_new)
        l_chunk = jnp.sum(jnp.sum(p_t.reshape(rows // 8, 8, ATTN_TQ), axis=0), axis=0, keepdims=True)
        v_t = jnp.concatenate([vlo_ref[:, pl.ds(start, rows)], vhi_ref[:, pl.ds(start, rows)]], axis=0)
        update = jnp.dot(v_t, p_t.astype(BF16), preferred_element_type=F32)
        if alpha is None:
            acc_ref[n] = update
            return l_chunk
        acc_ref[n] = alpha * acc_ref[n] + update
        return alpha * l + l_chunk

    own = pl.multiple_of(sup * ATTN_CHUNK, ATTN_CHUNK)
    m_chunk = scores_pass(0, own, 1, True)
    stats = []
    for n, (a, sub) in enumerate(chains):
        if n + 1 < n_chains:
            m_next = scores_pass(n + 1, own, chains[n + 1][0] + 1, True)
        else:
            m_next = scores_pass(0, 0, BLOCKS_PER_CHUNK, False)
        stats.append((m_chunk, values_pass(n, own, (a + 1) * ATTN_TQ, m_chunk, None, None)))
        m_chunk = m_next

    def body(c, carry):
        stats, m_chunk = carry
        start = pl.multiple_of(c * ATTN_CHUNK, ATTN_CHUNK)
        out = []
        for n in range(n_chains):
            if n + 1 < n_chains:
                m_next = scores_pass(n + 1, start, BLOCKS_PER_CHUNK, False)
            else:
                m_next = scores_pass(0, pl.multiple_of((c + 1) * ATTN_CHUNK, ATTN_CHUNK), BLOCKS_PER_CHUNK, False)
            m, l = stats[n]
            m_new = jnp.maximum(m, m_chunk)
            out.append((m_new, values_pass(n, start, ATTN_CHUNK, m_new, jnp.exp2(m - m_new), l)))
            m_chunk = m_next
        return tuple(out), m_chunk

    stats, _ = _past_chunks_loop(sup, body, (tuple(stats), m_chunk))

    lam = (jnp.exp(jnp.sum(lq1_ref[...] * lk1_ref[...], axis=1, keepdims=True))
           - jnp.exp(jnp.sum(lq2_ref[...] * lk2_ref[...], axis=1, keepdims=True)) + lambda_init)
    for a in range(TILES_PER_STEP):
        out = acc_ref[2 * a] / stats[2 * a][1] - lam * (acc_ref[2 * a + 1] / stats[2 * a + 1][1])
        inv = lax.rsqrt(jnp.mean(out * out, axis=0, keepdims=True) + NORM_EPS)
        out = out * inv * subg_ref[...]
        o_ref[a * ATTN_TQ:(a + 1) * ATTN_TQ, :] = (out * (1.0 - lambda_init)).T.astype(o_ref.dtype)


def _diff_attention(qb_t, kb, vb_t, lq1, lk1, lq2, lk2, sub_g_col, batch, seq, lambda_init):
    t = batch * seq
    n_steps = seq // ATTN_CHUNK
    q_spec = lambda c: pl.BlockSpec((None, HEAD_DIM, ATTN_CHUNK), lambda b, h, i: (2 * h + c, 0, b * n_steps + i))
    k_spec = lambda c: pl.BlockSpec((None, seq, HEAD_DIM), lambda b, h, i: (2 * h + c, b, 0))
    v_spec = lambda c: pl.BlockSpec((None, HEAD_DIM, seq), lambda b, h, i: (2 * h + c, 0, b))
    vec_spec = pl.BlockSpec((1, HEAD_DIM), lambda b, h, i: (0, 0))
    return pl.pallas_call(
        functools.partial(_diff_kernel, lambda_init=lambda_init),
        grid=(batch, DIFF_HEADS, n_steps),
        in_specs=[
            q_spec(0), q_spec(1), k_spec(0), k_spec(1), v_spec(0), v_spec(1),
            vec_spec, vec_spec, vec_spec, vec_spec,
            pl.BlockSpec((DIFF_V_DIM, 1), lambda b, h, i: (0, 0)),
        ],
        out_specs=pl.BlockSpec((ATTN_CHUNK, DIFF_V_DIM), lambda b, h, i: (b * n_steps + i, h)),
        out_shape=jax.ShapeDtypeStruct((t, DIFF_HEADS * DIFF_V_DIM), BF16),
        scratch_shapes=[pltpu.VMEM((2 * TILES_PER_STEP, DIFF_V_DIM, ATTN_TQ), F32)]
                       + [pltpu.VMEM((ATTN_CHUNK, ATTN_TQ), F32)] * (2 * TILES_PER_STEP),
        compiler_params=_params(3),
        name="diff_attn",
    )(qb_t, qb_t, kb, kb, vb_t, vb_t, lq1, lk1, lq2, lk2, sub_g_col)


def _mix_kernel(oa_ref, ob_ref, sga_ref, sgb_ref, x_ref, wm_ref, wd_ref, wo_ref, g_ref, o_ref):
    ya = jnp.dot(oa_ref[...], wm_ref[...], preferred_element_type=F32)
    yb = jnp.dot(ob_ref[...], wd_ref[...], preferred_element_type=F32)
    mixed = sga_ref[...] * ya + sgb_ref[...] * yb
    z = jnp.dot(mixed.astype(BF16), wo_ref[...], preferred_element_type=F32)
    o_ref[...] = x_ref[...] + z * _rms_scale(z) * g_ref[...]


def _mix(oa, ob, gates, x2, wm, wd, wo, g, tm=256):
    t = x2.shape[0]
    const = lambda shape: pl.BlockSpec(shape, lambda i: (0, 0))
    return pl.pallas_call(
        _mix_kernel,
        grid=(t // tm,),
        in_specs=[
            pl.BlockSpec((tm, SECTION), lambda i: (i, 0)),
            pl.BlockSpec((tm, SECTION), lambda i: (i, 0)),
            pl.BlockSpec((tm, D_MODEL), lambda i: (i, 0)),
            pl.BlockSpec((tm, D_MODEL), lambda i: (i, 1)),
            pl.BlockSpec((tm, D_MODEL), lambda i: (i, 0)),
            const((SECTION, D_MODEL)), const((SECTION, D_MODEL)), const((D_MODEL, D_MODEL)),
            const((1, D_MODEL)),
        ],
        out_specs=pl.BlockSpec((tm, D_MODEL), lambda i: (i, 0)),
        out_shape=jax.ShapeDtypeStruct((t, D_MODEL), F32),
        compiler_params=_params(1),
        name="mix_out",
    )(oa, ob, gates, gates, x2, wm, wd, wo, g)


def _mlp_kernel(h_ref, gpre_ref, wup_ref, wdown_ref, gpost_ref, o_ref, u_ref, acc_ref):
    k = pl.program_id(1)

    @pl.when(k == 0)
    def _():
        hf = h_ref[...]
        u_ref[...] = (hf * _rms_scale(hf) * gpre_ref[...]).astype(BF16)
        acc_ref[...] = jnp.zeros_like(acc_ref)

    a = jnp.dot(u_ref[...], wup_ref[...], preferred_element_type=F32)
    a = jnp.square(jnp.maximum(a, 0.0)).astype(BF16)
    acc_ref[...] += jnp.dot(a, wdown_ref[...], preferred_element_type=F32)

    @pl.when(k == pl.num_programs(1) - 1)
    def _():
        ff = acc_ref[...]
        o_ref[...] = h_ref[...] + ff * _rms_scale(ff) * gpost_ref[...]


def _mlp(h, gpre, wup, wdown, gpost, tm=512, tf=1024):
    t = h.shape[0]
    return pl.pallas_call(
        _mlp_kernel,
        grid=(t // tm, D_FF // tf),
        in_specs=[
            pl.BlockSpec((tm, D_MODEL), lambda i, k: (i, 0)),
            pl.BlockSpec((1, D_MODEL), lambda i, k: (0, 0)),
            pl.BlockSpec((D_MODEL, tf), lambda i, k: (0, k)),
            pl.BlockSpec((tf, D_MODEL), lambda i, k: (k, 0)),
            pl.BlockSpec((1, D_MODEL), lambda i, k: (0, 0)),
        ],
        out_specs=pl.BlockSpec((tm, D_MODEL), lambda i, k: (i, 0)),
        out_shape=jax.ShapeDtypeStruct((t, D_MODEL), F32),
        scratch_shapes=[pltpu.VMEM((tm, D_MODEL), BF16), pltpu.VMEM((tm, D_MODEL), F32)],
        compiler_params=_params(2),
        name="mlp",
    )(h, gpre, wup, wdown, gpost)


def _ple_kernel(h_ref, p_ref, gpre_ref, wgate_ref, wproj_ref, gpost_ref, o_ref):
    hf = h_ref[...]
    u = (hf * _rms_scale(hf) * gpre_ref[...]).astype(BF16)
    gate = jax.nn.sigmoid(jnp.dot(u, wgate_ref[...], preferred_element_type=F32))
    e = jnp.dot(p_ref[...].astype(BF16), wproj_ref[...], preferred_element_type=F32) * gate
    o_ref[...] = hf + e * _rms_scale(e) * gpost_ref[...]


def _ple(h, p2, gpre, wgate, wproj, gpost, tm=512):
    t = h.shape[0]
    const = lambda shape: pl.BlockSpec(shape, lambda i: (0, 0))
    return pl.pallas_call(
        _ple_kernel,
        grid=(t // tm,),
        in_specs=[
            pl.BlockSpec((tm, D_MODEL), lambda i: (i, 0)),
            pl.BlockSpec((tm, PLE_DIM), lambda i: (i, 0)),
            const((1, D_MODEL)), const((D_MODEL, D_MODEL)), const((PLE_DIM, D_MODEL)), const((1, D_MODEL)),
        ],
        out_specs=pl.BlockSpec((tm, D_MODEL), lambda i: (i, 0)),
        out_shape=jax.ShapeDtypeStruct((t, D_MODEL), F32),
        compiler_params=_params(1),
        name="ple",
    )(h, p2, gpre, wgate, wproj, gpost)


def _rope_tables(seq):
    half = ROT_DIM // 2
    pos_f = jnp.arange(seq, dtype=F32)
    inv_freq = 1.0 / (ROPE_THETA ** (jnp.arange(half, dtype=F32) * 2.0 / ROT_DIM))
    ang = pos_f[:, None] * inv_freq[None, :]
    cos, sin = jnp.cos(ang), jnp.sin(ang)
    pad = HEAD_DIM - ROT_DIM
    zeros = jnp.zeros((seq, half), F32)
    cos_t = jnp.concatenate([cos, cos, jnp.ones((seq, pad), F32)], axis=1)
    sin_lo = jnp.concatenate([-sin, zeros, jnp.zeros((seq, pad), F32)], axis=1)
    sin_hi = jnp.concatenate([zeros, sin, jnp.zeros((seq, pad), F32)], axis=1)
    return cos_t, sin_lo, sin_hi


def kernel(x, p, w_in, w_br_moba, w_br_diff, w_out, lambda_q1, lambda_k1, lambda_q2, lambda_k2, diff_subln_g,
           g_mix_pre, g_mix_post, w_up, w_down, g_mlp_pre, g_mlp_post, w_ple_proj, w_ple_gate, g_ple_pre,
           g_ple_post):
    batch, seq, _ = x.shape
    depth = w_in.shape[0]
    t = batch * seq
    n_blocks = seq // MOBA_BLOCK
    cos_t, sin_lo, sin_hi = _rope_tables(seq)
    row = lambda v: v.reshape(1, -1).astype(F32)

    h = x.reshape(t, D_MODEL)
    for layer in range(depth):
        lambda_init = 0.8 - 0.6 * math.exp(-0.3 * layer)
        w_l = w_in[layer].astype(BF16)
        qa_t, ka, va_t, qb_t, kb, vb_t, kmean = _qkv_proj(h, row(g_mix_pre[layer]), w_l[:, :QKV_WIDTH],
                                                          cos_t, sin_lo, sin_hi, seq)
        gates = _gate_proj(h, row(g_mix_pre[layer]), w_l[:, QKV_WIDTH:])
        kmean = kmean.transpose(1, 0, 2, 3).reshape(N_HEADS, batch * n_blocks, HEAD_DIM)
        oa = _moba_attention(qa_t, ka, va_t, kmean, batch, seq)
        ob = _diff_attention(qb_t, kb, vb_t, row(lambda_q1[layer]), row(lambda_k1[layer]), row(lambda_q2[layer]),
                             row(lambda_k2[layer]), diff_subln_g[layer].reshape(-1, 1).astype(F32),
                             batch, seq, lambda_init)
        h = _mix(oa, ob, gates, h, w_br_moba[layer].astype(BF16), w_br_diff[layer].astype(BF16),
                 w_out[layer].astype(BF16), row(g_mix_post[layer]))
        h = _mlp(h, row(g_mlp_pre[layer]), w_up[layer].astype(BF16), w_down[layer].astype(BF16),
                 row(g_mlp_post[layer]))
        h = _ple(h, p[layer].reshape(t, PLE_DIM), row(g_ple_pre[layer]), w_ple_gate[layer].astype(BF16),
                 w_ple_proj[layer].astype(BF16), row(g_ple_post[layer]))
    return h.reshape(batch, seq, D_MODEL)
```

```python
import functools
import math

import jax
import jax.numpy as jnp
from jax import lax
from jax.experimental import pallas as pl
from jax.experimental.pallas import tpu as pltpu

F32 = jnp.float32
BF16 = jnp.bfloat16

D_MODEL = 2048
HEAD_DIM = 128
N_HEADS = 8
MOBA_BLOCK = 256
MOBA_TOPK = 3
DIFF_HEADS = 4
DIFF_V_DIM = 2 * HEAD_DIM
ROT_DIM = HEAD_DIM // 4
ROPE_THETA = 500000.0
D_FF = 4 * D_MODEL
PLE_DIM = 256
NORM_EPS = 1e-6
SECTION = N_HEADS * HEAD_DIM
QKV_WIDTH = 6 * SECTION
GATE_WIDTH = 2 * D_MODEL

ATTN_TQ = 256
ATTN_CHUNK = 1024
BLOCKS_PER_CHUNK = ATTN_CHUNK // MOBA_BLOCK
TILES_PER_STEP = ATTN_CHUNK // ATTN_TQ
ONES_ROWS = 16
LOOP_UNROLLS = (4, 2, 1)

VMEM_LIMIT_BYTES = 56 * 1024 * 1024

NEG_INF = float("-inf")
NT_DIMS = (((1,), (1,)), ((), ()))
Q_SCALE = HEAD_DIM ** -0.5 * math.log2(math.e)


def _params(n_axes):
    return pltpu.CompilerParams(dimension_semantics=("arbitrary",) * n_axes,
                                vmem_limit_bytes=VMEM_LIMIT_BYTES)


def _rms_scale(xf):
    return lax.rsqrt(jnp.mean(xf * xf, axis=-1, keepdims=True) + NORM_EPS)


def _qkv_kernel(x_ref, g_ref, w_ref, cos_ref, sin_lo_ref, sin_hi_ref,
                qa_ref, ka_ref, va_ref, qb_ref, kb_ref, vb_ref, kmean_ref, *, tm):
    xf = x_ref[...]
    u = (xf * _rms_scale(xf) * g_ref[...]).astype(BF16)
    cos, sin_lo, sin_hi = cos_ref[...], sin_lo_ref[...], sin_hi_ref[...]

    def section(j):
        y = jnp.dot(u, w_ref[:, j * SECTION:(j + 1) * SECTION], preferred_element_type=F32)
        return [y[:, h * HEAD_DIM:(h + 1) * HEAD_DIM] for h in range(N_HEADS)]

    def rope(yh):
        from_hi = pltpu.roll(yh, HEAD_DIM - ROT_DIM // 2, axis=1)
        from_lo = pltpu.roll(yh, ROT_DIM // 2, axis=1)
        return yh * cos + from_hi * sin_lo + from_lo * sin_hi

    for h, yh in enumerate(section(0)):
        qa_ref[h] = (rope(yh) * Q_SCALE).T.astype(BF16)
    for h, yh in enumerate(section(1)):
        kr = rope(yh)
        ka_ref[h] = kr.astype(BF16)
        kmean_ref[h] = jnp.mean(kr.reshape(tm // MOBA_BLOCK, MOBA_BLOCK, HEAD_DIM), axis=1)
    ones_tile = jnp.where(lax.broadcasted_iota(jnp.int32, (ONES_ROWS, tm), 0) == 0, 1.0, 0.0).astype(BF16)
    for h, yh in enumerate(section(2)):
        va_ref[h, :HEAD_DIM] = yh.T.astype(BF16)
        va_ref[h, HEAD_DIM:] = ones_tile
    for h, yh in enumerate(section(3)):
        qb_ref[h] = (rope(yh) * Q_SCALE).T.astype(BF16)
    for h, yh in enumerate(section(4)):
        kb_ref[h] = rope(yh).astype(BF16)
    for h, yh in enumerate(section(5)):
        vb_ref[h] = yh.T.astype(BF16)


def _qkv_proj(x2, g, w_qkv, cos_t, sin_lo, sin_hi, seq, tm=512):
    t = x2.shape[0]
    n_rows = t // tm
    pos_blocks = seq // tm
    row_major = jax.ShapeDtypeStruct((N_HEADS, t, HEAD_DIM), BF16)
    col_major = jax.ShapeDtypeStruct((N_HEADS, HEAD_DIM, t), BF16)
    row_spec = pl.BlockSpec((N_HEADS, tm, HEAD_DIM), lambda i: (0, i, 0))
    col_spec = pl.BlockSpec((N_HEADS, HEAD_DIM, tm), lambda i: (0, 0, i))
    tab_spec = pl.BlockSpec((tm, HEAD_DIM), lambda i: (i % pos_blocks, 0))
    return pl.pallas_call(
        functools.partial(_qkv_kernel, tm=tm),
        grid=(n_rows,),
        in_specs=[
            pl.BlockSpec((tm, D_MODEL), lambda i: (i, 0)),
            pl.BlockSpec((1, D_MODEL), lambda i: (0, 0)),
            pl.BlockSpec((D_MODEL, QKV_WIDTH), lambda i: (0, 0), pipeline_mode=pl.Buffered(1)),
            tab_spec, tab_spec, tab_spec,
        ],
        out_specs=[
            col_spec, row_spec, pl.BlockSpec((N_HEADS, HEAD_DIM + ONES_ROWS, tm), lambda i: (0, 0, i)),
            col_spec, row_spec, col_spec,
            pl.BlockSpec((None, N_HEADS, tm // MOBA_BLOCK, HEAD_DIM), lambda i: (i, 0, 0, 0)),
        ],
        out_shape=[
            col_major, row_major, jax.ShapeDtypeStruct((N_HEADS, HEAD_DIM + ONES_ROWS, t), BF16),
            col_major, row_major, col_major,
            jax.ShapeDtypeStruct((n_rows, N_HEADS, tm // MOBA_BLOCK, HEAD_DIM), F32),
        ],
        compiler_params=_params(1),
        name="qkv_proj",
    )(x2, g, w_qkv, cos_t, sin_lo, sin_hi)


def _gate_kernel(x_ref, g_ref, w_ref, o_ref, *, tn):
    xf = x_ref[...]
    u = (xf * _rms_scale(xf) * g_ref[...]).astype(BF16)
    for j in range(GATE_WIDTH // tn):
        cols = slice(j * tn, (j + 1) * tn)
        o_ref[:, cols] = jax.nn.sigmoid(jnp.dot(u, w_ref[:, cols], preferred_element_type=F32)).astype(o_ref.dtype)


def _gate_proj(x2, g, w_gate, tm=512, tn=1024):
    t = x2.shape[0]
    return pl.pallas_call(
        functools.partial(_gate_kernel, tn=tn),
        grid=(t // tm,),
        in_specs=[
            pl.BlockSpec((tm, D_MODEL), lambda i: (i, 0)),
            pl.BlockSpec((1, D_MODEL), lambda i: (0, 0)),
            pl.BlockSpec((D_MODEL, GATE_WIDTH), lambda i: (0, 0), pipeline_mode=pl.Buffered(1)),
        ],
        out_specs=pl.BlockSpec((tm, GATE_WIDTH), lambda i: (i, 0)),
        out_shape=jax.ShapeDtypeStruct((t, GATE_WIDTH), BF16),
        compiler_params=_params(1),
        name="gate_proj",
    )(x2, g, w_gate)


def _past_chunks_loop(n_chunks, body, carry):
    done = 0
    for width in LOOP_UNROLLS:
        def group(i, carry, width=width, base=done):
            for u in range(width):
                carry = body(base + width * i + u, carry)
            return carry

        trips = (n_chunks - done) // width
        carry = lax.fori_loop(0, trips, group, carry)
        done = done + trips * width
    return carry


def _causal_bias():
    key = lax.broadcasted_iota(jnp.int32, (ATTN_TQ, ATTN_TQ), 0)
    qry = lax.broadcasted_iota(jnp.int32, (ATTN_TQ, ATTN_TQ), 1)
    return jnp.where(key <= qry, 0.0, NEG_INF)


def _moba_kernel(q_ref, k_ref, v_ref, kmean_ref, o_ref, acc_ref, s0_ref, s1_ref, s2_ref, s3_ref, *, n_blocks):
    sup = pl.program_id(2)
    q_all = q_ref[...]

    km = kmean_ref[...]
    km_hi = km.astype(BF16)
    km_mid = (km - km_hi.astype(F32)).astype(BF16)
    km_lo = (km - km_hi.astype(F32) - km_mid.astype(F32)).astype(BF16)
    gate = (jnp.dot(km_lo, q_all, preferred_element_type=F32) + jnp.dot(km_mid, q_all, preferred_element_type=F32)
            + jnp.dot(km_hi, q_all, preferred_element_type=F32))
    blk = lax.broadcasted_iota(jnp.int32, gate.shape, 0).astype(F32)
    own_blk = (sup * TILES_PER_STEP
               + lax.broadcasted_iota(jnp.int32, (1, ATTN_CHUNK), 1) // MOBA_BLOCK).astype(F32)
    g = jnp.where(blk < own_blk, gate, NEG_INF)
    picks = []
    for _ in range(MOBA_TOPK):
        best = jnp.max(g, axis=0, keepdims=True)
        first = jnp.min(jnp.where(g == best, blk, float(n_blocks)), axis=0, keepdims=True)
        first = jnp.where(best > NEG_INF, first, -1.0)
        picks.append(first)
        g = jnp.where(blk == first, NEG_INF, g)

    def lanes(a):
        return slice(a * ATTN_TQ, (a + 1) * ATTN_TQ)

    q_t = [q_all[:, lanes(a)] for a in range(TILES_PER_STEP)]
    tile_picks = [[p[:, lanes(a)] for p in picks] for a in range(TILES_PER_STEP)]
    s_refs = (s0_ref, s1_ref, s2_ref, s3_ref)
    causal_bias = _causal_bias()

    def selection_bias(a, block_index):
        jf = jnp.asarray(block_index).astype(F32)
        chosen = (tile_picks[a][0] == jf) | (tile_picks[a][1] == jf) | (tile_picks[a][2] == jf)
        return jnp.where(chosen, 0.0, NEG_INF)

    def past_biases(a, c):
        return [selection_bias(a, c * BLOCKS_PER_CHUNK + r) for r in range(BLOCKS_PER_CHUNK)]

    def own_biases(a):
        return [selection_bias(a, sup * TILES_PER_STEP + r) for r in range(a)] + [causal_bias]

    def scores_pass(a, start, biases):
        top = None
        for r, bias in enumerate(biases):
            k_r = k_ref[pl.ds(start + r * MOBA_BLOCK, MOBA_BLOCK), :]
            s_r = jnp.dot(k_r, q_t[a], preferred_element_type=F32)
            if bias.shape[0] != 1:
                s_r = s_r + bias
            s_refs[a][r * MOBA_BLOCK:(r + 1) * MOBA_BLOCK, :] = s_r
            part = jnp.max(s_r.reshape(MOBA_BLOCK // 8, 8, ATTN_TQ), axis=0)
            if bias.shape[0] == 1:
                part = part + bias
            top = part if top is None else jnp.maximum(top, part)
        return jnp.max(top, axis=0, keepdims=True)

    def values_pass(a, start, biases, m_new, alpha):
        p_blocks = []
        for r, bias in enumerate(biases):
            shift = m_new - bias if bias.shape[0] == 1 else m_new
            p_blocks.append(jnp.exp2(s_refs[a][r * MOBA_BLOCK:(r + 1) * MOBA_BLOCK, :] - shift).astype(BF16))
        rows = len(biases) * MOBA_BLOCK
        update = jnp.dot(v_ref[:, pl.ds(start, rows)], jnp.concatenate(p_blocks, axis=0),
                         preferred_element_type=F32)
        acc_ref[a] = update if alpha is None else alpha * acc_ref[a] + update

    own = pl.multiple_of(sup * ATTN_CHUNK, ATTN_CHUNK)
    m_chunk = scores_pass(0, own, own_biases(0))
    tops = []
    for a in range(TILES_PER_STEP):
        if a + 1 < TILES_PER_STEP:
            m_next = scores_pass(a + 1, own, own_biases(a + 1))
        else:
            m_next = scores_pass(0, 0, past_biases(0, 0))
        values_pass(a, own, own_biases(a), m_chunk, None)
        tops.append(m_chunk)
        m_chunk = m_next

    def body(c, carry):
        tops, m_chunk = carry
        start = pl.multiple_of(c * ATTN_CHUNK, ATTN_CHUNK)
        out = []
        for a in range(TILES_PER_STEP):
            if a + 1 < TILES_PER_STEP:
                m_next = scores_pass(a + 1, start, past_biases(a + 1, c))
            else:
                m_next = scores_pass(0, pl.multiple_of((c + 1) * ATTN_CHUNK, ATTN_CHUNK), past_biases(0, c + 1))
            m_new = jnp.maximum(tops[a], m_chunk)
            values_pass(a, start, past_biases(a, c), m_new, jnp.exp2(tops[a] - m_new))
            out.append(m_new)
            m_chunk = m_next
        return tuple(out), m_chunk

    _past_chunks_loop(sup, body, (tuple(tops), m_chunk))
    for a in range(TILES_PER_STEP):
        acc = acc_ref[a]
        o_ref[lanes(a), :] = (acc[:HEAD_DIM] / acc[HEAD_DIM:HEAD_DIM + 1]).T.astype(o_ref.dtype)


def _moba_attention(qa_t, ka, va_t, kmean, batch, seq):
    t = batch * seq
    n_blocks = seq // MOBA_BLOCK
    n_steps = seq // ATTN_CHUNK
    return pl.pallas_call(
        functools.partial(_moba_kernel, n_blocks=n_blocks),
        grid=(batch, N_HEADS, n_steps),
        in_specs=[
            pl.BlockSpec((None, HEAD_DIM, ATTN_CHUNK), lambda b, h, i: (h, 0, b * n_steps + i)),
            pl.BlockSpec((None, seq, HEAD_DIM), lambda b, h, i: (h, b, 0)),
            pl.BlockSpec((None, HEAD_DIM + ONES_ROWS, seq), lambda b, h, i: (h, 0, b)),
            pl.BlockSpec((None, n_blocks, HEAD_DIM), lambda b, h, i: (h, b, 0)),
        ],
        out_specs=pl.BlockSpec((ATTN_CHUNK, HEAD_DIM), lambda b, h, i: (b * n_steps + i, h)),
        out_shape=jax.ShapeDtypeStruct((t, SECTION), BF16),
        scratch_shapes=[pltpu.VMEM((TILES_PER_STEP, HEAD_DIM + ONES_ROWS, ATTN_TQ), F32)]
                       + [pltpu.VMEM((ATTN_CHUNK, ATTN_TQ), F32)] * TILES_PER_STEP,
        compiler_params=_params(3),
        name="moba_attn",
    )(qa_t, ka, va_t, kmean)


def _diff_kernel(q0_ref, q1_ref, k0_ref, k1_ref, vlo_ref, vhi_ref, lq1_ref, lk1_ref, lq2_ref, lk2_ref, subg_ref,
                 o_ref, acc_ref, *s_refs, lambda_init):
    sup = pl.program_id(2)
    q_refs = (q0_ref, q1_ref)
    k_refs = (k0_ref, k1_ref)
    chains = [(a, sub) for a in range(TILES_PER_STEP) for sub in range(2)]
    n_chains = len(chains)
    q_t = [q_refs[sub][:, a * ATTN_TQ:(a + 1) * ATTN_TQ] for a, sub in chains]
    causal_bias = _causal_bias()

    def scores_pass(n, start, n_blocks, causal_last):
        top = None
        for r in range(n_blocks):
            k_r = k_refs[chains[n][1]][pl.ds(start + r * ATTN_TQ, ATTN_TQ), :]
            s_r = jnp.dot(k_r, q_t[n], preferred_element_type=F32)
            if causal_last and r == n_blocks - 1:
                s_r = s_r + causal_bias
            s_refs[n][r * ATTN_TQ:(r + 1) * ATTN_TQ, :] = s_r
            part = jnp.max(s_r.reshape(ATTN_TQ // 8, 8, ATTN_TQ), axis=0)
            top = part if top is None else jnp.maximum(top, part)
        return jnp.max(top, axis=0, keepdims=True)

    def values_pass(n, start, rows, m_new, alpha, l):
        p_t = jnp.exp2(s_refs[n][:rows, :] - m_new)
        l_chunk = jnp.sum(jnp.sum(p_t.reshape(rows // 8, 8, ATTN_TQ), axis=0), axis=0, keepdims=True)
        v_t = jnp.concatenate([vlo_ref[:, pl.ds(start, rows)], vhi_ref[:, pl.ds(start, rows)]], axis=0)
        update = jnp.dot(v_t, p_t.astype(BF16), preferred_element_type=F32)
        if alpha is None:
            acc_ref[n] = update
            return l_chunk
        acc_ref[n] = alpha * acc_ref[n] + update
        return alpha * l + l_chunk

    own = pl.multiple_of(sup * ATTN_CHUNK, ATTN_CHUNK)
    m_chunk = scores_pass(0, own, 1, True)
    stats = []
    for n, (a, sub) in enumerate(chains):
        if n + 1 < n_chains:
            m_next = scores_pass(n + 1, own, chains[n + 1][0] + 1, True)
        else:
            m_next = scores_pass(0, 0, BLOCKS_PER_CHUNK, False)
        stats.append((m_chunk, values_pass(n, own, (a + 1) * ATTN_TQ, m_chunk, None, None)))
        m_chunk = m_next

    def body(c, carry):
        stats, m_chunk = carry
        start = pl.multiple_of(c * ATTN_CHUNK, ATTN_CHUNK)
        out = []
        for n in range(n_chains):
            if n + 1 < n_chains:
                m_next = scores_pass(n + 1, start, BLOCKS_PER_CHUNK, False)
            else:
                m_next = scores_pass(0, pl.multiple_of((c + 1) * ATTN_CHUNK, ATTN_CHUNK), BLOCKS_PER_CHUNK, False)
            m, l = stats[n]
            m_new = jnp.maximum(m, m_chunk)
            out.append((m_new, values_pass(n, start, ATTN_CHUNK, m_new, jnp.exp2(m - m_new), l)))
            m_chunk = m_next
        return tuple(out), m_chunk

    stats, _ = _past_chunks_loop(sup, body, (tuple(stats), m_chunk))

    lam = (jnp.exp(jnp.sum(lq1_ref[...] * lk1_ref[...], axis=1, keepdims=True))
           - jnp.exp(jnp.sum(lq2_ref[...] * lk2_ref[...], axis=1, keepdims=True)) + lambda_init)
    for a in range(TILES_PER_STEP):
        out = acc_ref[2 * a] / stats[2 * a][1] - lam * (acc_ref[2 * a + 1] / stats[2 * a + 1][1])
        inv = lax.rsqrt(jnp.mean(out * out, axis=0, keepdims=True) + NORM_EPS)
        out = out * inv * subg_ref[...]
        o_ref[a * ATTN_TQ:(a + 1) * ATTN_TQ, :] = (out * (1.0 - lambda_init)).T.astype(o_ref.dtype)


def _diff_attention(qb_t, kb, vb_t, lq1, lk1, lq2, lk2, sub_g_col, batch, seq, lambda_init):
    t = batch * seq
    n_steps = seq // ATTN_CHUNK
    q_spec = lambda c: pl.BlockSpec((None, HEAD_DIM, ATTN_CHUNK), lambda b, h, i: (2 * h + c, 0, b * n_steps + i))
    k_spec = lambda c: pl.BlockSpec((None, seq, HEAD_DIM), lambda b, h, i: (2 * h + c, b, 0))
    v_spec = lambda c: pl.BlockSpec((None, HEAD_DIM, seq), lambda b, h, i: (2 * h + c, 0, b))
    vec_spec = pl.BlockSpec((1, HEAD_DIM), lambda b, h, i: (0, 0))
    return pl.pallas_call(
        functools.partial(_diff_kernel, lambda_init=lambda_init),
        grid=(batch, DIFF_HEADS, n_steps),
        in_specs=[
            q_spec(0), q_spec(1), k_spec(0), k_spec(1), v_spec(0), v_spec(1),
            vec_spec, vec_spec, vec_spec, vec_spec,
            pl.BlockSpec((DIFF_V_DIM, 1), lambda b, h, i: (0, 0)),
        ],
        out_specs=pl.BlockSpec((ATTN_CHUNK, DIFF_V_DIM), lambda b, h, i: (b * n_steps + i, h)),
        out_shape=jax.ShapeDtypeStruct((t, DIFF_HEADS * DIFF_V_DIM), BF16),
        scratch_shapes=[pltpu.VMEM((2 * TILES_PER_STEP, DIFF_V_DIM, ATTN_TQ), F32)]
                       + [pltpu.VMEM((ATTN_CHUNK, ATTN_TQ), F32)] * (2 * TILES_PER_STEP),
        compiler_params=_params(3),
        name="diff_attn",
    )(qb_t, qb_t, kb, kb, vb_t, vb_t, lq1, lk1, lq2, lk2, sub_g_col)


def _mix_kernel(oa_ref, ob_ref, sga_ref, sgb_ref, x_ref, wm_ref, wd_ref, wo_ref, gpost_ref, gnext_ref,
                h_ref, u_ref):
    half = x_ref.shape[0] // 2
    for rows in (slice(0, half), slice(half, 2 * half)):
        ya = jnp.dot(oa_ref[rows, :], wm_ref[...], preferred_element_type=F32)
        yb = jnp.dot(ob_ref[rows, :], wd_ref[...], preferred_element_type=F32)
        mixed = sga_ref[rows, :].astype(F32) * ya + sgb_ref[rows, :].astype(F32) * yb
        z = jnp.dot(mixed.astype(BF16), wo_ref[...], preferred_element_type=F32)
        h = x_ref[rows, :] + z * _rms_scale(z) * gpost_ref[...]
        h_ref[rows, :] = h
        u_ref[rows, :] = (h * _rms_scale(h) * gnext_ref[...]).astype(BF16)


def _mix(oa, ob, gates, x2, wm, wd, wo, gpost, gnext, tm=512):
    t = x2.shape[0]
    const = lambda shape: pl.BlockSpec(shape, lambda i: (0, 0), pipeline_mode=pl.Buffered(1))
    row = lambda width, col=0: pl.BlockSpec((tm, width), lambda i: (i, col))
    return pl.pallas_call(
        _mix_kernel,
        grid=(t // tm,),
        in_specs=[
            row(SECTION), row(SECTION), row(D_MODEL, 0), row(D_MODEL, 1), row(D_MODEL),
            const((SECTION, D_MODEL)), const((SECTION, D_MODEL)), const((D_MODEL, D_MODEL)),
            const((1, D_MODEL)), const((1, D_MODEL)),
        ],
        out_specs=[row(D_MODEL), row(D_MODEL)],
        out_shape=[jax.ShapeDtypeStruct((t, D_MODEL), F32), jax.ShapeDtypeStruct((t, D_MODEL), BF16)],
        compiler_params=_params(1),
        name="mix_out",
    )(oa, ob, gates, gates, x2, wm, wd, wo, gpost, gnext)


def _mlp_kernel(u_ref, wup_ref, wdown_ref, o_ref):
    @pl.when(pl.program_id(1) == 0)
    def _():
        o_ref[...] = jnp.zeros_like(o_ref)

    a = jnp.dot(u_ref[...], wup_ref[...], preferred_element_type=F32)
    a = jnp.square(jnp.maximum(a, 0.0)).astype(BF16)
    o_ref[...] += jnp.dot(a, wdown_ref[...], preferred_element_type=F32)


def _mlp(u, wup, wdown, tm=1024, tf=1024):
    t = u.shape[0]
    return pl.pallas_call(
        _mlp_kernel,
        grid=(t // tm, D_FF // tf),
        in_specs=[
            pl.BlockSpec((tm, D_MODEL), lambda i, k: (i, 0)),
            pl.BlockSpec((D_MODEL, tf), lambda i, k: (0, k)),
            pl.BlockSpec((tf, D_MODEL), lambda i, k: (k, 0)),
        ],
        out_specs=pl.BlockSpec((tm, D_MODEL), lambda i, k: (i, 0)),
        out_shape=jax.ShapeDtypeStruct((t, D_MODEL), F32),
        compiler_params=_params(2),
        name="mlp",
    )(u, wup, wdown)


def _ple_kernel(h_ref, ff_ref, p_ref, gmlp_ref, gpre_ref, wgate_ref, wproj_ref, gpost_ref, o_ref):
    half = h_ref.shape[0] // 2
    halves = [slice(i * half, (i + 1) * half) for i in range(2)]
    normed = []
    for rows in halves:
        ff = ff_ref[rows, :]
        h = h_ref[rows, :] + ff * _rms_scale(ff) * gmlp_ref[...]
        normed.append((h, (h * _rms_scale(h) * gpre_ref[...]).astype(BF16)))
    for rows, (h, u) in zip(halves, normed):
        gate = jax.nn.sigmoid(jnp.dot(u, wgate_ref[...], preferred_element_type=F32))
        e = jnp.dot(p_ref[rows, :].astype(BF16), wproj_ref[...], preferred_element_type=F32) * gate
        o_ref[rows, :] = h + e * _rms_scale(e) * gpost_ref[...]


def _ple(h, ff, p2, gmlp, gpre, wgate, wproj, gpost, tm=512):
    t = h.shape[0]
    const = lambda shape: pl.BlockSpec(shape, lambda i: (0, 0), pipeline_mode=pl.Buffered(1))
    row = lambda width: pl.BlockSpec((tm, width), lambda i: (i, 0))
    return pl.pallas_call(
        _ple_kernel,
        grid=(t // tm,),
        in_specs=[
            row(D_MODEL), row(D_MODEL), row(PLE_DIM),
            const((1, D_MODEL)), const((1, D_MODEL)), const((D_MODEL, D_MODEL)), const((PLE_DIM, D_MODEL)),
            const((1, D_MODEL)),
        ],
        out_specs=row(D_MODEL),
        out_shape=jax.ShapeDtypeStruct((t, D_MODEL), F32),
        compiler_params=_params(1),
        name="ple",
    )(h, ff, p2, gmlp, gpre, wgate, wproj, gpost)


def _rope_tables(seq):
    half = ROT_DIM // 2
    pos_f = jnp.arange(seq, dtype=F32)
    inv_freq = 1.0 / (ROPE_THETA ** (jnp.arange(half, dtype=F32) * 2.0 / ROT_DIM))
    ang = pos_f[:, None] * inv_freq[None, :]
    cos, sin = jnp.cos(ang), jnp.sin(ang)
    pad = HEAD_DIM - ROT_DIM
    zeros = jnp.zeros((seq, half), F32)
    cos_t = jnp.concatenate([cos, cos, jnp.ones((seq, pad), F32)], axis=1)
    sin_lo = jnp.concatenate([-sin, zeros, jnp.zeros((seq, pad), F32)], axis=1)
    sin_hi = jnp.concatenate([zeros, sin, jnp.zeros((seq, pad), F32)], axis=1)
    return cos_t, sin_lo, sin_hi


def kernel(x, p, w_in, w_br_moba, w_br_diff, w_out, lambda_q1, lambda_k1, lambda_q2, lambda_k2, diff_subln_g,
           g_mix_pre, g_mix_post, w_up, w_down, g_mlp_pre, g_mlp_post, w_ple_proj, w_ple_gate, g_ple_pre,
           g_ple_post):
    batch, seq, _ = x.shape
    depth = w_in.shape[0]
    t = batch * seq
    n_blocks = seq // MOBA_BLOCK
    cos_t, sin_lo, sin_hi = _rope_tables(seq)
    row = lambda v: v.reshape(1, -1).astype(F32)

    h = x.reshape(t, D_MODEL)
    for layer in range(depth):
        lambda_init = 0.8 - 0.6 * math.exp(-0.3 * layer)
        w_l = w_in[layer].astype(BF16)
        qa_t, ka, va_t, qb_t, kb, vb_t, kmean = _qkv_proj(h, row(g_mix_pre[layer]), w_l[:, :QKV_WIDTH],
                                                          cos_t, sin_lo, sin_hi, seq)
        gates = _gate_proj(h, row(g_mix_pre[layer]), w_l[:, QKV_WIDTH:])
        kmean = kmean.transpose(1, 0, 2, 3).reshape(N_HEADS, batch * n_blocks, HEAD_DIM)
        oa = _moba_attention(qa_t, ka, va_t, kmean, batch, seq)
        ob = _diff_attention(qb_t, kb, vb_t, row(lambda_q1[layer]), row(lambda_k1[layer]), row(lambda_q2[layer]),
                             row(lambda_k2[layer]), diff_subln_g[layer].reshape(-1, 1).astype(F32),
                             batch, seq, lambda_init)
        h, u = _mix(oa, ob, gates, h, w_br_moba[layer].astype(BF16), w_br_diff[layer].astype(BF16),
                    w_out[layer].astype(BF16), row(g_mix_post[layer]), row(g_mlp_pre[layer]))
        ff = _mlp(u, w_up[layer].astype(BF16), w_down[layer].astype(BF16))
        h = _ple(h, ff, p[layer].reshape(t, PLE_DIM), row(g_mlp_post[layer]), row(g_ple_pre[layer]),
                 w_ple_gate[layer].astype(BF16), w_ple_proj[layer].astype(BF16), row(g_ple_post[layer]))
    return h.reshape(batch, seq, D_MODEL)
```

```python
import functools
import math

import jax
import jax.numpy as jnp
from jax import lax
from jax.experimental import pallas as pl
from jax.experimental.pallas import tpu as pltpu

F32 = jnp.float32
BF16 = jnp.bfloat16

D_MODEL = 2048
HEAD_DIM = 128
N_HEADS = 8
MOBA_BLOCK = 256
MOBA_TOPK = 3
DIFF_HEADS = 4
DIFF_V_DIM = 2 * HEAD_DIM
ROT_DIM = HEAD_DIM // 4
ROPE_THETA = 500000.0
D_FF = 4 * D_MODEL
PLE_DIM = 256
NORM_EPS = 1e-6
SECTION = N_HEADS * HEAD_DIM
QKV_WIDTH = 6 * SECTION
GATE_WIDTH = 2 * D_MODEL

ATTN_TQ = 256
ATTN_CHUNK = 1024
BLOCKS_PER_CHUNK = ATTN_CHUNK // MOBA_BLOCK
TILES_PER_STEP = ATTN_CHUNK // ATTN_TQ
SUBLANES = 8
ONES_ROWS = 2 * SUBLANES
LOOP_UNROLLS = (4, 2, 1)

VMEM_LIMIT_BYTES = 56 * 1024 * 1024

NEG_INF = float("-inf")
Q_SCALE = HEAD_DIM ** -0.5 * math.log2(math.e)


def _params(n_axes, fused_inputs=None):
    return pltpu.CompilerParams(dimension_semantics=("arbitrary",) * n_axes,
                                vmem_limit_bytes=VMEM_LIMIT_BYTES, allow_input_fusion=fused_inputs)


def _rms_scale(xf):
    return lax.rsqrt(jnp.mean(xf * xf, axis=-1, keepdims=True) + NORM_EPS)


def _qkv_kernel(x_ref, g_ref, w_ref, cos_ref, sin_lo_ref, sin_hi_ref,
                qa_ref, ka_ref, va_ref, qb_ref, kb_ref, vb_ref, kmean_ref, *, tm):
    xf = x_ref[...]
    u = (xf * _rms_scale(xf) * g_ref[...]).astype(BF16)
    cos, sin_lo, sin_hi = cos_ref[...], sin_lo_ref[...], sin_hi_ref[...]

    def section(j):
        y = jnp.dot(u, w_ref[:, j * SECTION:(j + 1) * SECTION], preferred_element_type=F32)
        return [y[:, h * HEAD_DIM:(h + 1) * HEAD_DIM] for h in range(N_HEADS)]

    def rope(yh):
        from_hi = pltpu.roll(yh, HEAD_DIM - ROT_DIM // 2, axis=1)
        from_lo = pltpu.roll(yh, ROT_DIM // 2, axis=1)
        return yh * cos + from_hi * sin_lo + from_lo * sin_hi

    for h, yh in enumerate(section(0)):
        qa_ref[h] = (rope(yh) * Q_SCALE).T.astype(BF16)
    for h, yh in enumerate(section(1)):
        kr = rope(yh)
        ka_ref[h] = kr.astype(BF16)
        kmean_ref[h] = jnp.mean(kr.reshape(tm // MOBA_BLOCK, MOBA_BLOCK, HEAD_DIM), axis=1)
    ones_tile = jnp.where(lax.broadcasted_iota(jnp.int32, (ONES_ROWS, tm), 0) == 0, 1.0, 0.0).astype(BF16)
    for h, yh in enumerate(section(2)):
        va_ref[h, :HEAD_DIM] = yh.T.astype(BF16)
        va_ref[h, HEAD_DIM:] = ones_tile
    for h, yh in enumerate(section(3)):
        qb_ref[h] = (rope(yh) * Q_SCALE).T.astype(BF16)
    for h, yh in enumerate(section(4)):
        kb_ref[h] = rope(yh).astype(BF16)
    for h, yh in enumerate(section(5)):
        vb_ref[h] = yh.T.astype(BF16)


def _qkv_proj(x2, g, w_qkv, cos_t, sin_lo, sin_hi, seq, tm=512):
    t = x2.shape[0]
    n_rows = t // tm
    pos_blocks = seq // tm
    row_major = jax.ShapeDtypeStruct((N_HEADS, t, HEAD_DIM), BF16)
    col_major = jax.ShapeDtypeStruct((N_HEADS, HEAD_DIM, t), BF16)
    row_spec = pl.BlockSpec((N_HEADS, tm, HEAD_DIM), lambda i: (0, i, 0))
    col_spec = pl.BlockSpec((N_HEADS, HEAD_DIM, tm), lambda i: (0, 0, i))
    tab_spec = pl.BlockSpec((tm, HEAD_DIM), lambda i: (i % pos_blocks, 0))
    return pl.pallas_call(
        functools.partial(_qkv_kernel, tm=tm),
        grid=(n_rows,),
        in_specs=[
            pl.BlockSpec((tm, D_MODEL), lambda i: (i, 0)),
            pl.BlockSpec((1, D_MODEL), lambda i: (0, 0)),
            pl.BlockSpec((D_MODEL, QKV_WIDTH), lambda i: (0, 0), pipeline_mode=pl.Buffered(1)),
            tab_spec, tab_spec, tab_spec,
        ],
        out_specs=[
            col_spec, row_spec, pl.BlockSpec((N_HEADS, HEAD_DIM + ONES_ROWS, tm), lambda i: (0, 0, i)),
            col_spec, row_spec, col_spec,
            pl.BlockSpec((None, N_HEADS, tm // MOBA_BLOCK, HEAD_DIM), lambda i: (i, 0, 0, 0)),
        ],
        out_shape=[
            col_major, row_major, jax.ShapeDtypeStruct((N_HEADS, HEAD_DIM + ONES_ROWS, t), BF16),
            col_major, row_major, col_major,
            jax.ShapeDtypeStruct((n_rows, N_HEADS, tm // MOBA_BLOCK, HEAD_DIM), F32),
        ],
        compiler_params=_params(1, [False, False, True, False, False, False]),
        name="qkv_proj",
    )(x2, g, w_qkv, cos_t, sin_lo, sin_hi)


def _gate_kernel(x_ref, g_ref, w_ref, o_ref, *, tn):
    xf = x_ref[...]
    u = (xf * _rms_scale(xf) * g_ref[...]).astype(BF16)
    for j in range(GATE_WIDTH // tn):
        cols = slice(j * tn, (j + 1) * tn)
        o_ref[:, cols] = jax.nn.sigmoid(jnp.dot(u, w_ref[:, cols], preferred_element_type=F32)).astype(o_ref.dtype)


def _gate_proj(x2, g, w_gate, tm=512, tn=1024):
    t = x2.shape[0]
    return pl.pallas_call(
        functools.partial(_gate_kernel, tn=tn),
        grid=(t // tm,),
        in_specs=[
            pl.BlockSpec((tm, D_MODEL), lambda i: (i, 0)),
            pl.BlockSpec((1, D_MODEL), lambda i: (0, 0)),
            pl.BlockSpec((D_MODEL, GATE_WIDTH), lambda i: (0, 0), pipeline_mode=pl.Buffered(1)),
        ],
        out_specs=pl.BlockSpec((tm, GATE_WIDTH), lambda i: (i, 0)),
        out_shape=jax.ShapeDtypeStruct((t, GATE_WIDTH), BF16),
        compiler_params=_params(1, [False, False, True]),
        name="gate_proj",
    )(x2, g, w_gate)


def _past_chunks_loop(n_chunks, body, carry):
    done = 0
    for width in LOOP_UNROLLS:
        def group(i, carry, width=width, base=done):
            for u in range(width):
                carry = body(base + width * i + u, carry)
            return carry

        trips = (n_chunks - done) // width
        carry = lax.fori_loop(0, trips, group, carry)
        done = done + trips * width
    return carry


def _causal_bias():
    key = lax.broadcasted_iota(jnp.int32, (ATTN_TQ, ATTN_TQ), 0)
    qry = lax.broadcasted_iota(jnp.int32, (ATTN_TQ, ATTN_TQ), 1)
    return jnp.where(key <= qry, 0.0, NEG_INF)


def _moba_kernel(q_ref, k_ref, v_ref, kmean_ref, o_ref, acc_ref, s0_ref, s1_ref, s2_ref, s3_ref, *, n_blocks):
    sup = pl.program_id(2)
    q_all = q_ref[...]

    km = kmean_ref[...]
    km_hi = km.astype(BF16)
    km_mid = (km - km_hi.astype(F32)).astype(BF16)
    km_lo = (km - km_hi.astype(F32) - km_mid.astype(F32)).astype(BF16)
    gate = (jnp.dot(km_lo, q_all, preferred_element_type=F32) + jnp.dot(km_mid, q_all, preferred_element_type=F32)
            + jnp.dot(km_hi, q_all, preferred_element_type=F32))
    blk = lax.broadcasted_iota(jnp.int32, gate.shape, 0).astype(F32)
    own_blk = (sup * TILES_PER_STEP
               + lax.broadcasted_iota(jnp.int32, (1, ATTN_CHUNK), 1) // MOBA_BLOCK).astype(F32)
    g = jnp.where(blk < own_blk, gate, NEG_INF)
    picks = []
    for _ in range(MOBA_TOPK):
        best = jnp.max(g, axis=0, keepdims=True)
        first = jnp.min(jnp.where(g == best, blk, float(n_blocks)), axis=0, keepdims=True)
        first = jnp.where(best > NEG_INF, first, -1.0)
        picks.append(first)
        g = jnp.where(blk == first, NEG_INF, g)

    def lanes(a):
        return slice(a * ATTN_TQ, (a + 1) * ATTN_TQ)

    q_t = [q_all[:, lanes(a)] for a in range(TILES_PER_STEP)]
    tile_picks = [[p[:, lanes(a)] for p in picks] for a in range(TILES_PER_STEP)]
    s_refs = (s0_ref, s1_ref, s2_ref, s3_ref)
    causal_bias = _causal_bias()

    def selection_bias(a, block_index):
        jf = jnp.asarray(block_index).astype(F32)
        chosen = (tile_picks[a][0] == jf) | (tile_picks[a][1] == jf) | (tile_picks[a][2] == jf)
        return jnp.where(chosen, 0.0, NEG_INF)

    def past_biases(a, c):
        return [selection_bias(a, c * BLOCKS_PER_CHUNK + r) for r in range(BLOCKS_PER_CHUNK)]

    def own_biases(a):
        return [selection_bias(a, sup * TILES_PER_STEP + r) for r in range(a)] + [causal_bias]

    def scores_pass(a, start, biases):
        top = None
        for r, bias in enumerate(biases):
            k_r = k_ref[pl.ds(start + r * MOBA_BLOCK, MOBA_BLOCK), :]
            s_r = jnp.dot(k_r, q_t[a], preferred_element_type=F32)
            if bias.shape[0] != 1:
                s_r = s_r + bias
            s_refs[a][r * MOBA_BLOCK:(r + 1) * MOBA_BLOCK, :] = s_r
            part = jnp.max(s_r.reshape(MOBA_BLOCK // SUBLANES, SUBLANES, ATTN_TQ), axis=0)
            if bias.shape[0] == 1:
                part = part + bias
            top = part if top is None else jnp.maximum(top, part)
        return jnp.max(top, axis=0, keepdims=True)

    def values_pass(a, start, biases, m_new, alpha):
        p_blocks = []
        for r, bias in enumerate(biases):
            shift = m_new - bias if bias.shape[0] == 1 else m_new
            p_blocks.append(jnp.exp2(s_refs[a][r * MOBA_BLOCK:(r + 1) * MOBA_BLOCK, :] - shift).astype(BF16))
        rows = len(biases) * MOBA_BLOCK
        update = jnp.dot(v_ref[:, pl.ds(start, rows)], jnp.concatenate(p_blocks, axis=0),
                         preferred_element_type=F32)
        acc_ref[a] = update if alpha is None else alpha * acc_ref[a] + update

    own = pl.multiple_of(sup * ATTN_CHUNK, ATTN_CHUNK)
    m_chunk = scores_pass(0, own, own_biases(0))
    tops = []
    for a in range(TILES_PER_STEP):
        if a + 1 < TILES_PER_STEP:
            m_next = scores_pass(a + 1, own, own_biases(a + 1))
        else:
            m_next = scores_pass(0, 0, past_biases(0, 0))
        values_pass(a, own, own_biases(a), m_chunk, None)
        tops.append(m_chunk)
        m_chunk = m_next

    def body(c, carry):
        tops, m_chunk = carry
        start = pl.multiple_of(c * ATTN_CHUNK, ATTN_CHUNK)
        out = []
        for a in range(TILES_PER_STEP):
            if a + 1 < TILES_PER_STEP:
                m_next = scores_pass(a + 1, start, past_biases(a + 1, c))
            else:
                m_next = scores_pass(0, pl.multiple_of((c + 1) * ATTN_CHUNK, ATTN_CHUNK), past_biases(0, c + 1))
            m_new = jnp.maximum(tops[a], m_chunk)
            values_pass(a, start, past_biases(a, c), m_new, jnp.exp2(tops[a] - m_new))
            out.append(m_new)
            m_chunk = m_next
        return tuple(out), m_chunk

    _past_chunks_loop(sup, body, (tuple(tops), m_chunk))
    for a in range(TILES_PER_STEP):
        acc = acc_ref[a]
        o_ref[lanes(a), :] = (acc[:HEAD_DIM] / acc[HEAD_DIM:HEAD_DIM + 1]).T.astype(o_ref.dtype)


def _moba_attention(qa_t, ka, va_t, kmean, batch, seq):
    t = batch * seq
    n_blocks = seq // MOBA_BLOCK
    n_steps = seq // ATTN_CHUNK
    return pl.pallas_call(
        functools.partial(_moba_kernel, n_blocks=n_blocks),
        grid=(batch, N_HEADS, n_steps),
        in_specs=[
            pl.BlockSpec((None, HEAD_DIM, ATTN_CHUNK), lambda b, h, i: (h, 0, b * n_steps + i)),
            pl.BlockSpec((None, seq, HEAD_DIM), lambda b, h, i: (h, b, 0)),
            pl.BlockSpec((None, HEAD_DIM + ONES_ROWS, seq), lambda b, h, i: (h, 0, b)),
            pl.BlockSpec((None, n_blocks, HEAD_DIM), lambda b, h, i: (h, b, 0)),
        ],
        out_specs=pl.BlockSpec((ATTN_CHUNK, HEAD_DIM), lambda b, h, i: (b * n_steps + i, h)),
        out_shape=jax.ShapeDtypeStruct((t, SECTION), BF16),
        scratch_shapes=[pltpu.VMEM((TILES_PER_STEP, HEAD_DIM + ONES_ROWS, ATTN_TQ), F32)]
                       + [pltpu.VMEM((ATTN_CHUNK, ATTN_TQ), F32)] * TILES_PER_STEP,
        compiler_params=_params(3),
        name="moba_attn",
    )(qa_t, ka, va_t, kmean)


def _diff_kernel(q0_ref, q1_ref, k0_ref, k1_ref, vlo_ref, vhi_ref, lq1_ref, lk1_ref, lq2_ref, lk2_ref, subg_ref,
                 o_ref, acc_ref, *s_refs, lambda_init):
    sup = pl.program_id(2)
    q_refs = (q0_ref, q1_ref)
    k_refs = (k0_ref, k1_ref)
    chains = [(a, sub) for a in range(TILES_PER_STEP) for sub in range(2)]
    n_chains = len(chains)
    q_t = [q_refs[sub][:, a * ATTN_TQ:(a + 1) * ATTN_TQ] for a, sub in chains]
    causal_bias = _causal_bias()

    def scores_pass(n, start, n_blocks, causal_last):
        top = None
        for r in range(n_blocks):
            k_r = k_refs[chains[n][1]][pl.ds(start + r * ATTN_TQ, ATTN_TQ), :]
            s_r = jnp.dot(k_r, q_t[n], preferred_element_type=F32)
            if causal_last and r == n_blocks - 1:
                s_r = s_r + causal_bias
            s_refs[n][r * ATTN_TQ:(r + 1) * ATTN_TQ, :] = s_r
            part = jnp.max(s_r.reshape(ATTN_TQ // SUBLANES, SUBLANES, ATTN_TQ), axis=0)
            top = part if top is None else jnp.maximum(top, part)
        return jnp.max(top, axis=0, keepdims=True)

    def values_pass(n, start, rows, m_new, alpha, l):
        p_t = jnp.exp2(s_refs[n][:rows, :] - m_new)
        l_chunk = jnp.sum(jnp.sum(p_t.reshape(rows // SUBLANES, SUBLANES, ATTN_TQ), axis=0), axis=0, keepdims=True)
        v_t = jnp.concatenate([vlo_ref[:, pl.ds(start, rows)], vhi_ref[:, pl.ds(start, rows)]], axis=0)
        update = jnp.dot(v_t, p_t.astype(BF16), preferred_element_type=F32)
        if alpha is None:
            acc_ref[n] = update
            return l_chunk
        acc_ref[n] = alpha * acc_ref[n] + update
        return alpha * l + l_chunk

    own = pl.multiple_of(sup * ATTN_CHUNK, ATTN_CHUNK)
    m_chunk = scores_pass(0, own, 1, True)
    stats = []
    for n, (a, sub) in enumerate(chains):
        if n + 1 < n_chains:
            m_next = scores_pass(n + 1, own, chains[n + 1][0] + 1, True)
        else:
            m_next = scores_pass(0, 0, BLOCKS_PER_CHUNK, False)
        stats.append((m_chunk, values_pass(n, own, (a + 1) * ATTN_TQ, m_chunk, None, None)))
        m_chunk = m_next

    def body(c, carry):
        stats, m_chunk = carry
        start = pl.multiple_of(c * ATTN_CHUNK, ATTN_CHUNK)
        out = []
        for n in range(n_chains):
            if n + 1 < n_chains:
                m_next = scores_pass(n + 1, start, BLOCKS_PER_CHUNK, False)
            else:
                m_next = scores_pass(0, pl.multiple_of((c + 1) * ATTN_CHUNK, ATTN_CHUNK), BLOCKS_PER_CHUNK, False)
            m, l = stats[n]
            m_new = jnp.maximum(m, m_chunk)
            out.append((m_new, values_pass(n, start, ATTN_CHUNK, m_new, jnp.exp2(m - m_new), l)))
            m_chunk = m_next
        return tuple(out), m_chunk

    stats, _ = _past_chunks_loop(sup, body, (tuple(stats), m_chunk))

    lam = (jnp.exp(jnp.sum(lq1_ref[...] * lk1_ref[...], axis=1, keepdims=True))
           - jnp.exp(jnp.sum(lq2_ref[...] * lk2_ref[...], axis=1, keepdims=True)) + lambda_init)
    for a in range(TILES_PER_STEP):
        out = acc_ref[2 * a] / stats[2 * a][1] - lam * (acc_ref[2 * a + 1] / stats[2 * a + 1][1])
        inv = lax.rsqrt(jnp.mean(out * out, axis=0, keepdims=True) + NORM_EPS)
        out = out * inv * subg_ref[...]
        o_ref[a * ATTN_TQ:(a + 1) * ATTN_TQ, :] = (out * (1.0 - lambda_init)).T.astype(o_ref.dtype)


def _diff_attention(qb_t, kb, vb_t, lq1, lk1, lq2, lk2, sub_g_col, batch, seq, lambda_init):
    t = batch * seq
    n_steps = seq // ATTN_CHUNK
    q_spec = lambda c: pl.BlockSpec((None, HEAD_DIM, ATTN_CHUNK), lambda b, h, i: (2 * h + c, 0, b * n_steps + i))
    k_spec = lambda c: pl.BlockSpec((None, seq, HEAD_DIM), lambda b, h, i: (2 * h + c, b, 0))
    v_spec = lambda c: pl.BlockSpec((None, HEAD_DIM, seq), lambda b, h, i: (2 * h + c, 0, b))
    vec_spec = pl.BlockSpec((1, HEAD_DIM), lambda b, h, i: (0, 0))
    return pl.pallas_call(
        functools.partial(_diff_kernel, lambda_init=lambda_init),
        grid=(batch, DIFF_HEADS, n_steps),
        in_specs=[
            q_spec(0), q_spec(1), k_spec(0), k_spec(1), v_spec(0), v_spec(1),
            vec_spec, vec_spec, vec_spec, vec_spec,
            pl.BlockSpec((DIFF_V_DIM, 1), lambda b, h, i: (0, 0)),
        ],
        out_specs=pl.BlockSpec((ATTN_CHUNK, DIFF_V_DIM), lambda b, h, i: (b * n_steps + i, h)),
        out_shape=jax.ShapeDtypeStruct((t, DIFF_HEADS * DIFF_V_DIM), BF16),
        scratch_shapes=[pltpu.VMEM((2 * TILES_PER_STEP, DIFF_V_DIM, ATTN_TQ), F32)]
                       + [pltpu.VMEM((ATTN_CHUNK, ATTN_TQ), F32)] * (2 * TILES_PER_STEP),
        compiler_params=_params(3),
        name="diff_attn",
    )(qb_t, qb_t, kb, kb, vb_t, vb_t, lq1, lk1, lq2, lk2, sub_g_col)


def _mix_kernel(oa_ref, ob_ref, sga_ref, sgb_ref, x_ref, wm_ref, wd_ref, wo_ref, gpost_ref, gnext_ref,
                h_ref, u_ref):
    half = x_ref.shape[0] // 2
    for rows in (slice(0, half), slice(half, 2 * half)):
        ya = jnp.dot(oa_ref[rows, :], wm_ref[...], preferred_element_type=F32)
        yb = jnp.dot(ob_ref[rows, :], wd_ref[...], preferred_element_type=F32)
        mixed = sga_ref[rows, :].astype(F32) * ya + sgb_ref[rows, :].astype(F32) * yb
        z = jnp.dot(mixed.astype(BF16), wo_ref[...], preferred_element_type=F32)
        h = x_ref[rows, :] + z * _rms_scale(z) * gpost_ref[...]
        h_ref[rows, :] = h
        u_ref[rows, :] = (h * _rms_scale(h) * gnext_ref[...]).astype(BF16)


def _mix(oa, ob, gates, x2, wm, wd, wo, gpost, gnext, tm=512):
    t = x2.shape[0]
    const = lambda shape: pl.BlockSpec(shape, lambda i: (0, 0), pipeline_mode=pl.Buffered(1))
    row = lambda width, col=0: pl.BlockSpec((tm, width), lambda i: (i, col))
    return pl.pallas_call(
        _mix_kernel,
        grid=(t // tm,),
        in_specs=[
            row(SECTION), row(SECTION), row(D_MODEL, 0), row(D_MODEL, 1), row(D_MODEL),
            const((SECTION, D_MODEL)), const((SECTION, D_MODEL)), const((D_MODEL, D_MODEL)),
            const((1, D_MODEL)), const((1, D_MODEL)),
        ],
        out_specs=[row(D_MODEL), row(D_MODEL)],
        out_shape=[jax.ShapeDtypeStruct((t, D_MODEL), F32), jax.ShapeDtypeStruct((t, D_MODEL), BF16)],
        compiler_params=_params(1, [False] * 5 + [True] * 3 + [False] * 2),
        name="mix_out",
    )(oa, ob, gates, gates, x2, wm, wd, wo, gpost, gnext)


def _mlp_kernel(u_ref, wup_ref, wdown_ref, o_ref):
    @pl.when(pl.program_id(1) == 0)
    def _():
        o_ref[...] = jnp.zeros_like(o_ref)

    a = jnp.dot(u_ref[...], wup_ref[...], preferred_element_type=F32)
    a = jnp.square(jnp.maximum(a, 0.0)).astype(BF16)
    o_ref[...] += jnp.dot(a, wdown_ref[...], preferred_element_type=F32)


def _mlp(u, wup, wdown, tm=1024, tf=1024):
    t = u.shape[0]
    return pl.pallas_call(
        _mlp_kernel,
        grid=(t // tm, D_FF // tf),
        in_specs=[
            pl.BlockSpec((tm, D_MODEL), lambda i, k: (i, 0)),
            pl.BlockSpec((D_MODEL, tf), lambda i, k: (0, k)),
            pl.BlockSpec((tf, D_MODEL), lambda i, k: (k, 0)),
        ],
        out_specs=pl.BlockSpec((tm, D_MODEL), lambda i, k: (i, 0)),
        out_shape=jax.ShapeDtypeStruct((t, D_MODEL), F32),
        compiler_params=_params(2),
        name="mlp",
    )(u, wup, wdown)


def _ple_kernel(h_ref, ff_ref, p_ref, gmlp_ref, gpre_ref, wgate_ref, wproj_ref, gpost_ref, o_ref):
    half = h_ref.shape[0] // 2
    halves = [slice(i * half, (i + 1) * half) for i in range(2)]
    normed = []
    for rows in halves:
        ff = ff_ref[rows, :]
        h = h_ref[rows, :] + ff * _rms_scale(ff) * gmlp_ref[...]
        normed.append((h, (h * _rms_scale(h) * gpre_ref[...]).astype(BF16)))
    for rows, (h, u) in zip(halves, normed):
        gate = jax.nn.sigmoid(jnp.dot(u, wgate_ref[...], preferred_element_type=F32))
        e = jnp.dot(p_ref[rows, :].astype(BF16), wproj_ref[...], preferred_element_type=F32) * gate
        o_ref[rows, :] = h + e * _rms_scale(e) * gpost_ref[...]


def _ple(h, ff, p2, gmlp, gpre, wgate, wproj, gpost, tm=512):
    t = h.shape[0]
    const = lambda shape: pl.BlockSpec(shape, lambda i: (0, 0), pipeline_mode=pl.Buffered(1))
    row = lambda width: pl.BlockSpec((tm, width), lambda i: (i, 0))
    return pl.pallas_call(
        _ple_kernel,
        grid=(t // tm,),
        in_specs=[
            row(D_MODEL), row(D_MODEL), row(PLE_DIM),
            const((1, D_MODEL)), const((1, D_MODEL)), const((D_MODEL, D_MODEL)), const((PLE_DIM, D_MODEL)),
            const((1, D_MODEL)),
        ],
        out_specs=row(D_MODEL),
        out_shape=jax.ShapeDtypeStruct((t, D_MODEL), F32),
        compiler_params=_params(1, [False] * 5 + [True] * 2 + [False]),
        name="ple",
    )(h, ff, p2, gmlp, gpre, wgate, wproj, gpost)


def _rope_tables(seq):
    half = ROT_DIM // 2
    pos_f = jnp.arange(seq, dtype=F32)
    inv_freq = 1.0 / (ROPE_THETA ** (jnp.arange(half, dtype=F32) * 2.0 / ROT_DIM))
    ang = pos_f[:, None] * inv_freq[None, :]
    cos, sin = jnp.cos(ang), jnp.sin(ang)
    pad = HEAD_DIM - ROT_DIM
    zeros = jnp.zeros((seq, half), F32)
    cos_t = jnp.concatenate([cos, cos, jnp.ones((seq, pad), F32)], axis=1)
    sin_lo = jnp.concatenate([-sin, zeros, jnp.zeros((seq, pad), F32)], axis=1)
    sin_hi = jnp.concatenate([zeros, sin, jnp.zeros((seq, pad), F32)], axis=1)
    return cos_t, sin_lo, sin_hi


def kernel(x, p, w_in, w_br_moba, w_br_diff, w_out, lambda_q1, lambda_k1, lambda_q2, lambda_k2, diff_subln_g,
           g_mix_pre, g_mix_post, w_up, w_down, g_mlp_pre, g_mlp_post, w_ple_proj, w_ple_gate, g_ple_pre,
           g_ple_post):
    batch, seq, _ = x.shape
    depth = w_in.shape[0]
    t = batch * seq
    n_blocks = seq // MOBA_BLOCK
    cos_t, sin_lo, sin_hi = _rope_tables(seq)
    row = lambda v: v.reshape(1, -1).astype(F32)

    h = x.reshape(t, D_MODEL)
    for layer in range(depth):
        lambda_init = 0.8 - 0.6 * math.exp(-0.3 * layer)
        w_l = w_in[layer].astype(BF16)
        qa_t, ka, va_t, qb_t, kb, vb_t, kmean = _qkv_proj(h, row(g_mix_pre[layer]), w_l[:, :QKV_WIDTH],
                                                          cos_t, sin_lo, sin_hi, seq)
        gates = _gate_proj(h, row(g_mix_pre[layer]), w_l[:, QKV_WIDTH:])
        kmean = kmean.transpose(1, 0, 2, 3).reshape(N_HEADS, batch * n_blocks, HEAD_DIM)
        oa = _moba_attention(qa_t, ka, va_t, kmean, batch, seq)
        ob = _diff_attention(qb_t, kb, vb_t, row(lambda_q1[layer]), row(lambda_k1[layer]), row(lambda_q2[layer]),
                             row(lambda_k2[layer]), diff_subln_g[layer].reshape(-1, 1).astype(F32),
                             batch, seq, lambda_init)
        h, u = _mix(oa, ob, gates, h, w_br_moba[layer].astype(BF16), w_br_diff[layer].astype(BF16),
                    w_out[layer].astype(BF16), row(g_mix_post[layer]), row(g_mlp_pre[layer]))
        ff = _mlp(u, w_up[layer].astype(BF16), w_down[layer].astype(BF16))
        h = _ple(h, ff, p[layer].reshape(t, PLE_DIM), row(g_mlp_post[layer]), row(g_ple_pre[layer]),
                 w_ple_gate[layer].astype(BF16), w_ple_proj[layer].astype(BF16), row(g_ple_post[layer]))
    return h.reshape(batch, seq, D_MODEL)
```

```python
import functools
import math

import jax
import jax.numpy as jnp
from jax import lax
from jax.experimental import pallas as pl
from jax.experimental.pallas import tpu as pltpu

F32 = jnp.float32
BF16 = jnp.bfloat16

D_MODEL = 2048
HEAD_DIM = 128
N_HEADS = 8
MOBA_BLOCK = 256
MOBA_TOPK = 3
DIFF_HEADS = 4
DIFF_V_DIM = 2 * HEAD_DIM
ROT_DIM = HEAD_DIM // 4
ROPE_THETA = 500000.0
D_FF = 4 * D_MODEL
PLE_DIM = 256
NORM_EPS = 1e-6
SECTION = N_HEADS * HEAD_DIM
QKV_WIDTH = 6 * SECTION
GATE_WIDTH = 2 * D_MODEL

ATTN_TQ = 256
ATTN_CHUNK = 1024
BLOCKS_PER_CHUNK = ATTN_CHUNK // MOBA_BLOCK
TILES_PER_STEP = ATTN_CHUNK // ATTN_TQ
SUBLANES = 8
ONES_ROWS = 2 * SUBLANES
LOOP_UNROLLS = (4, 2, 1)

VMEM_LIMIT_BYTES = 56 * 1024 * 1024

NEG_INF = float("-inf")
Q_SCALE = HEAD_DIM ** -0.5 * math.log2(math.e)


def _params(n_axes):
    return pltpu.CompilerParams(dimension_semantics=("arbitrary",) * n_axes,
                                vmem_limit_bytes=VMEM_LIMIT_BYTES)


def _rms_scale(xf):
    return lax.rsqrt(jnp.mean(xf * xf, axis=-1, keepdims=True) + NORM_EPS)


def _normed_halves(x_ref, g_ref):
    half = x_ref.shape[0] // 2
    out = []
    for rows in (slice(0, half), slice(half, 2 * half)):
        xf = x_ref[rows, :]
        out.append((xf * _rms_scale(xf) * g_ref[...]).astype(BF16))
    return out


def _dot_halves(u_halves, w):
    return jnp.concatenate([jnp.dot(u, w, preferred_element_type=F32) for u in u_halves], axis=0)


def _qkv_kernel(x_ref, g_ref, w_ref, cos_ref, sin_lo_ref, sin_hi_ref,
                qa_ref, ka_ref, va_ref, qb_ref, kb_ref, vb_ref, kmean_ref, *, tm):
    u_halves = _normed_halves(x_ref, g_ref)
    cos, sin_lo, sin_hi = cos_ref[...], sin_lo_ref[...], sin_hi_ref[...]

    def section(j):
        y = _dot_halves(u_halves, w_ref[:, j * SECTION:(j + 1) * SECTION])
        return [y[:, h * HEAD_DIM:(h + 1) * HEAD_DIM] for h in range(N_HEADS)]

    def rope(yh):
        from_hi = pltpu.roll(yh, HEAD_DIM - ROT_DIM // 2, axis=1)
        from_lo = pltpu.roll(yh, ROT_DIM // 2, axis=1)
        return yh * cos + from_hi * sin_lo + from_lo * sin_hi

    for h, yh in enumerate(section(0)):
        qa_ref[h] = (rope(yh) * Q_SCALE).T.astype(BF16)
    for h, yh in enumerate(section(1)):
        kr = rope(yh)
        ka_ref[h] = kr.astype(BF16)
        kmean_ref[h] = jnp.mean(kr.reshape(tm // MOBA_BLOCK, MOBA_BLOCK, HEAD_DIM), axis=1)
    ones_tile = jnp.where(lax.broadcasted_iota(jnp.int32, (ONES_ROWS, tm), 0) == 0, 1.0, 0.0).astype(BF16)
    for h, yh in enumerate(section(2)):
        va_ref[h, :HEAD_DIM] = yh.T.astype(BF16)
        va_ref[h, HEAD_DIM:] = ones_tile
    for h, yh in enumerate(section(3)):
        qb_ref[h] = (rope(yh) * Q_SCALE).T.astype(BF16)
    for h, yh in enumerate(section(4)):
        kb_ref[h] = rope(yh).astype(BF16)
    for h, yh in enumerate(section(5)):
        vb_ref[h] = yh.T.astype(BF16)


def _qkv_proj(x2, g, w_all, cos_t, sin_lo, sin_hi, seq, tm=512):
    t = x2.shape[0]
    n_rows = t // tm
    pos_blocks = seq // tm
    row_major = jax.ShapeDtypeStruct((N_HEADS, t, HEAD_DIM), BF16)
    col_major = jax.ShapeDtypeStruct((N_HEADS, HEAD_DIM, t), BF16)
    row_spec = pl.BlockSpec((N_HEADS, tm, HEAD_DIM), lambda i: (0, i, 0))
    col_spec = pl.BlockSpec((N_HEADS, HEAD_DIM, tm), lambda i: (0, 0, i))
    tab_spec = pl.BlockSpec((tm, HEAD_DIM), lambda i: (i % pos_blocks, 0))
    return pl.pallas_call(
        functools.partial(_qkv_kernel, tm=tm),
        grid=(n_rows,),
        in_specs=[
            pl.BlockSpec((tm, D_MODEL), lambda i: (i, 0)),
            pl.BlockSpec((1, D_MODEL), lambda i: (0, 0)),
            pl.BlockSpec((D_MODEL, QKV_WIDTH), lambda i: (0, 0), pipeline_mode=pl.Buffered(1)),
            tab_spec, tab_spec, tab_spec,
        ],
        out_specs=[
            col_spec, row_spec, pl.BlockSpec((N_HEADS, HEAD_DIM + ONES_ROWS, tm), lambda i: (0, 0, i)),
            col_spec, row_spec, col_spec,
            pl.BlockSpec((None, N_HEADS, tm // MOBA_BLOCK, HEAD_DIM), lambda i: (i, 0, 0, 0)),
        ],
        out_shape=[
            col_major, row_major, jax.ShapeDtypeStruct((N_HEADS, HEAD_DIM + ONES_ROWS, t), BF16),
            col_major, row_major, col_major,
            jax.ShapeDtypeStruct((n_rows, N_HEADS, tm // MOBA_BLOCK, HEAD_DIM), F32),
        ],
        compiler_params=_params(1),
        name="qkv_proj",
    )(x2, g, w_all, cos_t, sin_lo, sin_hi)


def _gate_kernel(x_ref, g_ref, wa_ref, wb_ref, o_ref, *, tn):
    u_halves = _normed_halves(x_ref, g_ref)
    for n, w_ref in enumerate((wa_ref, wb_ref)):
        for j in range(D_MODEL // tn):
            y = _dot_halves(u_halves, w_ref[:, j * tn:(j + 1) * tn])
            o_ref[:, n * D_MODEL + j * tn:n * D_MODEL + (j + 1) * tn] = jax.nn.sigmoid(y).astype(o_ref.dtype)


def _gate_proj(x2, g, w_all, tm=512, tn=1024):
    t = x2.shape[0]
    first = QKV_WIDTH // D_MODEL
    gate_w = lambda n: pl.BlockSpec((D_MODEL, D_MODEL), lambda i: (0, first + n), pipeline_mode=pl.Buffered(1))
    return pl.pallas_call(
        functools.partial(_gate_kernel, tn=tn),
        grid=(t // tm,),
        in_specs=[
            pl.BlockSpec((tm, D_MODEL), lambda i: (i, 0)),
            pl.BlockSpec((1, D_MODEL), lambda i: (0, 0)),
            gate_w(0), gate_w(1),
        ],
        out_specs=pl.BlockSpec((tm, GATE_WIDTH), lambda i: (i, 0)),
        out_shape=jax.ShapeDtypeStruct((t, GATE_WIDTH), BF16),
        compiler_params=_params(1),
        name="gate_proj",
    )(x2, g, w_all, w_all)


def _past_chunks_loop(n_chunks, body, carry):
    done = 0
    for width in LOOP_UNROLLS:
        def group(i, carry, width=width, base=done):
            for u in range(width):
                carry = body(base + width * i + u, carry)
            return carry

        trips = (n_chunks - done) // width
        carry = lax.fori_loop(0, trips, group, carry)
        done = done + trips * width
    return carry


def _causal_bias():
    key = lax.broadcasted_iota(jnp.int32, (ATTN_TQ, ATTN_TQ), 0)
    qry = lax.broadcasted_iota(jnp.int32, (ATTN_TQ, ATTN_TQ), 1)
    return jnp.where(key <= qry, 0.0, NEG_INF)


def _moba_kernel(q_ref, k_ref, v_ref, kmean_ref, o_ref, acc_ref, s0_ref, s1_ref, s2_ref, s3_ref, *, n_blocks):
    sup = pl.program_id(2)
    q_all = q_ref[...]

    km = kmean_ref[...]
    km_hi = km.astype(BF16)
    km_mid = (km - km_hi.astype(F32)).astype(BF16)
    km_lo = (km - km_hi.astype(F32) - km_mid.astype(F32)).astype(BF16)
    gate = (jnp.dot(km_lo, q_all, preferred_element_type=F32) + jnp.dot(km_mid, q_all, preferred_element_type=F32)
            + jnp.dot(km_hi, q_all, preferred_element_type=F32))
    blk = lax.broadcasted_iota(jnp.int32, gate.shape, 0).astype(F32)
    own_blk = (sup * TILES_PER_STEP
               + lax.broadcasted_iota(jnp.int32, (1, ATTN_CHUNK), 1) // MOBA_BLOCK).astype(F32)
    g = jnp.where(blk < own_blk, gate, NEG_INF)
    picks = []
    for _ in range(MOBA_TOPK):
        best = jnp.max(g, axis=0, keepdims=True)
        first = jnp.min(jnp.where(g == best, blk, float(n_blocks)), axis=0, keepdims=True)
        first = jnp.where(best > NEG_INF, first, -1.0)
        picks.append(first)
        g = jnp.where(blk == first, NEG_INF, g)

    def lanes(a):
        return slice(a * ATTN_TQ, (a + 1) * ATTN_TQ)

    q_t = [q_all[:, lanes(a)] for a in range(TILES_PER_STEP)]
    tile_picks = [[p[:, lanes(a)] for p in picks] for a in range(TILES_PER_STEP)]
    s_refs = (s0_ref, s1_ref, s2_ref, s3_ref)
    causal_bias = _causal_bias()

    def selection_bias(a, block_index):
        jf = jnp.asarray(block_index).astype(F32)
        chosen = (tile_picks[a][0] == jf) | (tile_picks[a][1] == jf) | (tile_picks[a][2] == jf)
        return jnp.where(chosen, 0.0, NEG_INF)

    def past_biases(a, c):
        return [selection_bias(a, c * BLOCKS_PER_CHUNK + r) for r in range(BLOCKS_PER_CHUNK)]

    def own_biases(a):
        return [selection_bias(a, sup * TILES_PER_STEP + r) for r in range(a)] + [causal_bias]

    def scores_pass(a, start, biases):
        top = None
        for r, bias in enumerate(biases):
            k_r = k_ref[pl.ds(start + r * MOBA_BLOCK, MOBA_BLOCK), :]
            s_r = jnp.dot(k_r, q_t[a], preferred_element_type=F32)
            if bias.shape[0] != 1:
                s_r = s_r + bias
            s_refs[a][r * MOBA_BLOCK:(r + 1) * MOBA_BLOCK, :] = s_r
            part = jnp.max(s_r.reshape(MOBA_BLOCK // SUBLANES, SUBLANES, ATTN_TQ), axis=0)
            if bias.shape[0] == 1:
                part = part + bias
            top = part if top is None else jnp.maximum(top, part)
        return jnp.max(top, axis=0, keepdims=True)

    def values_pass(a, start, biases, m_new, alpha):
        p_blocks = []
        for r, bias in enumerate(biases):
            shift = m_new - bias if bias.shape[0] == 1 else m_new
            p_blocks.append(jnp.exp2(s_refs[a][r * MOBA_BLOCK:(r + 1) * MOBA_BLOCK, :] - shift).astype(BF16))
        rows = len(biases) * MOBA_BLOCK
        update = jnp.dot(v_ref[:, pl.ds(start, rows)], jnp.concatenate(p_blocks, axis=0),
                         preferred_element_type=F32)
        acc_ref[a] = update if alpha is None else alpha * acc_ref[a] + update

    own = pl.multiple_of(sup * ATTN_CHUNK, ATTN_CHUNK)
    m_chunk = scores_pass(0, own, own_biases(0))
    tops = []
    for a in range(TILES_PER_STEP):
        if a + 1 < TILES_PER_STEP:
            m_next = scores_pass(a + 1, own, own_biases(a + 1))
        else:
            m_next = scores_pass(0, 0, past_biases(0, 0))
        values_pass(a, own, own_biases(a), m_chunk, None)
        tops.append(m_chunk)
        m_chunk = m_next

    def body(c, carry):
        tops, m_chunk = carry
        start = pl.multiple_of(c * ATTN_CHUNK, ATTN_CHUNK)
        out = []
        for a in range(TILES_PER_STEP):
            if a + 1 < TILES_PER_STEP:
                m_next = scores_pass(a + 1, start, past_biases(a + 1, c))
            else:
                m_next = scores_pass(0, pl.multiple_of((c + 1) * ATTN_CHUNK, ATTN_CHUNK), past_biases(0, c + 1))
            m_new = jnp.maximum(tops[a], m_chunk)
            values_pass(a, start, past_biases(a, c), m_new, jnp.exp2(tops[a] - m_new))
            out.append(m_new)
            m_chunk = m_next
        return tuple(out), m_chunk

    _past_chunks_loop(sup, body, (tuple(tops), m_chunk))
    for a in range(TILES_PER_STEP):
        acc = acc_ref[a]
        o_ref[lanes(a), :] = (acc[:HEAD_DIM] * (1.0 / acc[HEAD_DIM:HEAD_DIM + 1])).T.astype(o_ref.dtype)


def _moba_attention(qa_t, ka, va_t, kmean, batch, seq):
    t = batch * seq
    n_blocks = seq // MOBA_BLOCK
    n_steps = seq // ATTN_CHUNK
    return pl.pallas_call(
        functools.partial(_moba_kernel, n_blocks=n_blocks),
        grid=(batch, N_HEADS, n_steps),
        in_specs=[
            pl.BlockSpec((None, HEAD_DIM, ATTN_CHUNK), lambda b, h, i: (h, 0, b * n_steps + i)),
            pl.BlockSpec((None, seq, HEAD_DIM), lambda b, h, i: (h, b, 0)),
            pl.BlockSpec((None, HEAD_DIM + ONES_ROWS, seq), lambda b, h, i: (h, 0, b)),
            pl.BlockSpec((None, n_blocks, HEAD_DIM), lambda b, h, i: (h, b, 0)),
        ],
        out_specs=pl.BlockSpec((ATTN_CHUNK, HEAD_DIM), lambda b, h, i: (b * n_steps + i, h)),
        out_shape=jax.ShapeDtypeStruct((t, SECTION), BF16),
        scratch_shapes=[pltpu.VMEM((TILES_PER_STEP, HEAD_DIM + ONES_ROWS, ATTN_TQ), F32)]
                       + [pltpu.VMEM((ATTN_CHUNK, ATTN_TQ), F32)] * TILES_PER_STEP,
        compiler_params=_params(3),
        name="moba_attn",
    )(qa_t, ka, va_t, kmean)


def _diff_kernel(q0_ref, q1_ref, k0_ref, k1_ref, vlo_ref, vhi_ref, lq1_ref, lk1_ref, lq2_ref, lk2_ref, subg_ref,
                 o_ref, acc_ref, *s_refs, lambda_init):
    sup = pl.program_id(2)
    q_refs = (q0_ref, q1_ref)
    k_refs = (k0_ref, k1_ref)
    chains = [(a, sub) for a in range(TILES_PER_STEP) for sub in range(2)]
    n_chains = len(chains)
    q_t = [q_refs[sub][:, a * ATTN_TQ:(a + 1) * ATTN_TQ] for a, sub in chains]
    causal_bias = _causal_bias()

    def scores_pass(n, start, n_blocks, causal_last):
        top = None
        for r in range(n_blocks):
            k_r = k_refs[chains[n][1]][pl.ds(start + r * ATTN_TQ, ATTN_TQ), :]
            s_r = jnp.dot(k_r, q_t[n], preferred_element_type=F32)
            if causal_last and r == n_blocks - 1:
                s_r = s_r + causal_bias
            s_refs[n][r * ATTN_TQ:(r + 1) * ATTN_TQ, :] = s_r
            part = jnp.max(s_r.reshape(ATTN_TQ // SUBLANES, SUBLANES, ATTN_TQ), axis=0)
            top = part if top is None else jnp.maximum(top, part)
        return jnp.max(top, axis=0, keepdims=True)

    def values_pass(n, start, rows, m_new, alpha, l):
        p_t = jnp.exp2(s_refs[n][:rows, :] - m_new)
        l_chunk = jnp.sum(jnp.sum(p_t.reshape(rows // SUBLANES, SUBLANES, ATTN_TQ), axis=0), axis=0, keepdims=True)
        v_t = jnp.concatenate([vlo_ref[:, pl.ds(start, rows)], vhi_ref[:, pl.ds(start, rows)]], axis=0)
        update = jnp.dot(v_t, p_t.astype(BF16), preferred_element_type=F32)
        if alpha is None:
            acc_ref[n] = update
            return l_chunk
        acc_ref[n] = alpha * acc_ref[n] + update
        return alpha * l + l_chunk

    own = pl.multiple_of(sup * ATTN_CHUNK, ATTN_CHUNK)
    m_chunk = scores_pass(0, own, 1, True)
    stats = []
    for n, (a, sub) in enumerate(chains):
        if n + 1 < n_chains:
            m_next = scores_pass(n + 1, own, chains[n + 1][0] + 1, True)
        else:
            m_next = scores_pass(0, 0, BLOCKS_PER_CHUNK, False)
        stats.append((m_chunk, values_pass(n, own, (a + 1) * ATTN_TQ, m_chunk, None, None)))
        m_chunk = m_next

    def body(c, carry):
        stats, m_chunk = carry
        start = pl.multiple_of(c * ATTN_CHUNK, ATTN_CHUNK)
        out = []
        for n in range(n_chains):
            if n + 1 < n_chains:
                m_next = scores_pass(n + 1, start, BLOCKS_PER_CHUNK, False)
            else:
                m_next = scores_pass(0, pl.multiple_of((c + 1) * ATTN_CHUNK, ATTN_CHUNK), BLOCKS_PER_CHUNK, False)
            m, l = stats[n]
            m_new = jnp.maximum(m, m_chunk)
            out.append((m_new, values_pass(n, start, ATTN_CHUNK, m_new, jnp.exp2(m - m_new), l)))
            m_chunk = m_next
        return tuple(out), m_chunk

    stats, _ = _past_chunks_loop(sup, body, (tuple(stats), m_chunk))

    lam = (jnp.exp(jnp.sum(lq1_ref[...] * lk1_ref[...], axis=1, keepdims=True))
           - jnp.exp(jnp.sum(lq2_ref[...] * lk2_ref[...], axis=1, keepdims=True)) + lambda_init)
    for a in range(TILES_PER_STEP):
        out = (acc_ref[2 * a] * (1.0 / stats[2 * a][1])
               - acc_ref[2 * a + 1] * (lam / stats[2 * a + 1][1]))
        inv = lax.rsqrt(jnp.mean(out * out, axis=0, keepdims=True) + NORM_EPS)
        out = out * inv * subg_ref[...]
        o_ref[a * ATTN_TQ:(a + 1) * ATTN_TQ, :] = (out * (1.0 - lambda_init)).T.astype(o_ref.dtype)


def _diff_attention(qb_t, kb, vb_t, lq1, lk1, lq2, lk2, sub_g_col, batch, seq, lambda_init):
    t = batch * seq
    n_steps = seq // ATTN_CHUNK
    q_spec = lambda c: pl.BlockSpec((None, HEAD_DIM, ATTN_CHUNK), lambda b, h, i: (2 * h + c, 0, b * n_steps + i))
    k_spec = lambda c: pl.BlockSpec((None, seq, HEAD_DIM), lambda b, h, i: (2 * h + c, b, 0))
    v_spec = lambda c: pl.BlockSpec((None, HEAD_DIM, seq), lambda b, h, i: (2 * h + c, 0, b))
    vec_spec = pl.BlockSpec((1, HEAD_DIM), lambda b, h, i: (0, 0))
    return pl.pallas_call(
        functools.partial(_diff_kernel, lambda_init=lambda_init),
        grid=(batch, DIFF_HEADS, n_steps),
        in_specs=[
            q_spec(0), q_spec(1), k_spec(0), k_spec(1), v_spec(0), v_spec(1),
            vec_spec, vec_spec, vec_spec, vec_spec,
            pl.BlockSpec((DIFF_V_DIM, 1), lambda b, h, i: (0, 0)),
        ],
        out_specs=pl.BlockSpec((ATTN_CHUNK, DIFF_V_DIM), lambda b, h, i: (b * n_steps + i, h)),
        out_shape=jax.ShapeDtypeStruct((t, DIFF_HEADS * DIFF_V_DIM), BF16),
        scratch_shapes=[pltpu.VMEM((2 * TILES_PER_STEP, DIFF_V_DIM, ATTN_TQ), F32)]
                       + [pltpu.VMEM((ATTN_CHUNK, ATTN_TQ), F32)] * (2 * TILES_PER_STEP),
        compiler_params=_params(3),
        name="diff_attn",
    )(qb_t, qb_t, kb, kb, vb_t, vb_t, lq1, lk1, lq2, lk2, sub_g_col)


def _mix_kernel(oa_ref, ob_ref, sga_ref, sgb_ref, x_ref, wm_ref, wd_ref, wo_ref, gpost_ref, gnext_ref,
                h_ref, u_ref):
    half = x_ref.shape[0] // 2
    for rows in (slice(0, half), slice(half, 2 * half)):
        ya = jnp.dot(oa_ref[rows, :], wm_ref[...], preferred_element_type=F32)
        yb = jnp.dot(ob_ref[rows, :], wd_ref[...], preferred_element_type=F32)
        mixed = sga_ref[rows, :].astype(F32) * ya + sgb_ref[rows, :].astype(F32) * yb
        z = jnp.dot(mixed.astype(BF16), wo_ref[...], preferred_element_type=F32)
        h = x_ref[rows, :] + z * _rms_scale(z) * gpost_ref[...]
        h_ref[rows, :] = h
        u_ref[rows, :] = (h * _rms_scale(h) * gnext_ref[...]).astype(BF16)


def _mix(oa, ob, gates, x2, wm, wd, wo, gpost, gnext, tm=512):
    t = x2.shape[0]
    const = lambda shape: pl.BlockSpec(shape, lambda i: (0, 0), pipeline_mode=pl.Buffered(1))
    row = lambda width, col=0: pl.BlockSpec((tm, width), lambda i: (i, col))
    return pl.pallas_call(
        _mix_kernel,
        grid=(t // tm,),
        in_specs=[
            row(SECTION), row(SECTION), row(D_MODEL, 0), row(D_MODEL, 1), row(D_MODEL),
            const((SECTION, D_MODEL)), const((SECTION, D_MODEL)), const((D_MODEL, D_MODEL)),
            const((1, D_MODEL)), const((1, D_MODEL)),
        ],
        out_specs=[row(D_MODEL), row(D_MODEL)],
        out_shape=[jax.ShapeDtypeStruct((t, D_MODEL), F32), jax.ShapeDtypeStruct((t, D_MODEL), BF16)],
        compiler_params=_params(1),
        name="mix_out",
    )(oa, ob, gates, gates, x2, wm, wd, wo, gpost, gnext)


def _mlp_kernel(u_ref, wup_ref, wdown_ref, o_ref):
    @pl.when(pl.program_id(1) == 0)
    def _():
        o_ref[...] = jnp.zeros_like(o_ref)

    a = jnp.dot(u_ref[...], wup_ref[...], preferred_element_type=F32)
    a = jnp.square(jnp.maximum(a, 0.0)).astype(BF16)
    o_ref[...] += jnp.dot(a, wdown_ref[...], preferred_element_type=F32)


def _mlp(u, wup, wdown, tm=1024, tf=1024):
    t = u.shape[0]
    return pl.pallas_call(
        _mlp_kernel,
        grid=(t // tm, D_FF // tf),
        in_specs=[
            pl.BlockSpec((tm, D_MODEL), lambda i, k: (i, 0)),
            pl.BlockSpec((D_MODEL, tf), lambda i, k: (0, k)),
            pl.BlockSpec((tf, D_MODEL), lambda i, k: (k, 0)),
        ],
        out_specs=pl.BlockSpec((tm, D_MODEL), lambda i, k: (i, 0)),
        out_shape=jax.ShapeDtypeStruct((t, D_MODEL), F32),
        compiler_params=_params(2),
        name="mlp",
    )(u, wup, wdown)


def _ple_kernel(h_ref, ff_ref, p_ref, gmlp_ref, gpre_ref, wgate_ref, wproj_ref, gpost_ref, o_ref):
    half = h_ref.shape[0] // 2
    halves = [slice(i * half, (i + 1) * half) for i in range(2)]
    normed = []
    for rows in halves:
        ff = ff_ref[rows, :]
        h = h_ref[rows, :] + ff * _rms_scale(ff) * gmlp_ref[...]
        normed.append((h, (h * _rms_scale(h) * gpre_ref[...]).astype(BF16)))
    for rows, (h, u) in zip(halves, normed):
        gate = jax.nn.sigmoid(jnp.dot(u, wgate_ref[...], preferred_element_type=F32))
        e = jnp.dot(p_ref[rows, :].astype(BF16), wproj_ref[...], preferred_element_type=F32) * gate
        o_ref[rows, :] = h + e * _rms_scale(e) * gpost_ref[...]


def _ple(h, ff, p2, gmlp, gpre, wgate, wproj, gpost, tm=512):
    t = h.shape[0]
    const = lambda shape: pl.BlockSpec(shape, lambda i: (0, 0), pipeline_mode=pl.Buffered(1))
    row = lambda width: pl.BlockSpec((tm, width), lambda i: (i, 0))
    return pl.pallas_call(
        _ple_kernel,
        grid=(t // tm,),
        in_specs=[
            row(D_MODEL), row(D_MODEL), row(PLE_DIM),
            const((1, D_MODEL)), const((1, D_MODEL)), const((D_MODEL, D_MODEL)), const((PLE_DIM, D_MODEL)),
            const((1, D_MODEL)),
        ],
        out_specs=row(D_MODEL),
        out_shape=jax.ShapeDtypeStruct((t, D_MODEL), F32),
        compiler_params=_params(1),
        name="ple",
    )(h, ff, p2, gmlp, gpre, wgate, wproj, gpost)


def _rope_tables(seq):
    half = ROT_DIM // 2
    pos_f = jnp.arange(seq, dtype=F32)
    inv_freq = 1.0 / (ROPE_THETA ** (jnp.arange(half, dtype=F32) * 2.0 / ROT_DIM))
    ang = pos_f[:, None] * inv_freq[None, :]
    cos, sin = jnp.cos(ang), jnp.sin(ang)
    pad = HEAD_DIM - ROT_DIM
    zeros = jnp.zeros((seq, half), F32)
    cos_t = jnp.concatenate([cos, cos, jnp.ones((seq, pad), F32)], axis=1)
    sin_lo = jnp.concatenate([-sin, zeros, jnp.zeros((seq, pad), F32)], axis=1)
    sin_hi = jnp.concatenate([zeros, sin, jnp.zeros((seq, pad), F32)], axis=1)
    return cos_t, sin_lo, sin_hi


def kernel(x, p, w_in, w_br_moba, w_br_diff, w_out, lambda_q1, lambda_k1, lambda_q2, lambda_k2, diff_subln_g,
           g_mix_pre, g_mix_post, w_up, w_down, g_mlp_pre, g_mlp_post, w_ple_proj, w_ple_gate, g_ple_pre,
           g_ple_post):
    batch, seq, _ = x.shape
    depth = w_in.shape[0]
    t = batch * seq
    n_blocks = seq // MOBA_BLOCK
    cos_t, sin_lo, sin_hi = _rope_tables(seq)
    row = lambda v: v.reshape(1, -1).astype(F32)

    h = x.reshape(t, D_MODEL)
    for layer in range(depth):
        lambda_init = 0.8 - 0.6 * math.exp(-0.3 * layer)
        w_l = w_in[layer].astype(BF16)
        qa_t, ka, va_t, qb_t, kb, vb_t, kmean = _qkv_proj(h, row(g_mix_pre[layer]), w_l, cos_t, sin_lo, sin_hi, seq)
        gates = _gate_proj(h, row(g_mix_pre[layer]), w_l)
        kmean = kmean.transpose(1, 0, 2, 3).reshape(N_HEADS, batch * n_blocks, HEAD_DIM)
        oa = _moba_attention(qa_t, ka, va_t, kmean, batch, seq)
        ob = _diff_attention(qb_t, kb, vb_t, row(lambda_q1[layer]), row(lambda_k1[layer]), row(lambda_q2[layer]),
                             row(lambda_k2[layer]), diff_subln_g[layer].reshape(-1, 1).astype(F32),
                             batch, seq, lambda_init)
        h, u = _mix(oa, ob, gates, h, w_br_moba[layer].astype(BF16), w_br_diff[layer].astype(BF16),
                    w_out[layer].astype(BF16), row(g_mix_post[layer]), row(g_mlp_pre[layer]))
        ff = _mlp(u, w_up[layer].astype(BF16), w_down[layer].astype(BF16))
        h = _ple(h, ff, p[layer].reshape(t, PLE_DIM), row(g_mlp_post[layer]), row(g_ple_pre[layer]),
                 w_ple_gate[layer].astype(BF16), w_ple_proj[layer].astype(BF16), row(g_ple_post[layer]))
    return h.reshape(batch, seq, D_MODEL)
```

```python
import functools
import math

import jax
import jax.numpy as jnp
import numpy as np
from jax import lax
from jax.experimental import pallas as pl
from jax.experimental.pallas import tpu as pltpu

F32 = jnp.float32
BF16 = jnp.bfloat16

D_MODEL = 2048
HEAD_DIM = 128
N_HEADS = 8
MOBA_BLOCK = 256
MOBA_TOPK = 3
DIFF_HEADS = 4
DIFF_V_DIM = 2 * HEAD_DIM
ROT_DIM = HEAD_DIM // 4
ROPE_THETA = 500000.0
D_FF = 4 * D_MODEL
PLE_DIM = 256
NORM_EPS = 1e-6
SECTION = N_HEADS * HEAD_DIM
QKV_WIDTH = 6 * SECTION
GATE_WIDTH = 2 * D_MODEL

ATTN_TQ = 256
ATTN_CHUNK = 1024
BLOCKS_PER_CHUNK = ATTN_CHUNK // MOBA_BLOCK
TILES_PER_STEP = ATTN_CHUNK // ATTN_TQ
SUBLANES = 8
ONES_ROWS = 2 * SUBLANES
LOOP_UNROLLS = (4, 2, 1)

VMEM_LIMIT_BYTES = 56 * 1024 * 1024

NEG_INF = float("-inf")
Q_SCALE = HEAD_DIM ** -0.5 * math.log2(math.e)


def _params(n_axes):
    return pltpu.CompilerParams(dimension_semantics=("arbitrary",) * n_axes,
                                vmem_limit_bytes=VMEM_LIMIT_BYTES)


def _rms_scale(xf):
    return lax.rsqrt(jnp.mean(xf * xf, axis=-1, keepdims=True) + NORM_EPS)


def _normed_halves(x_ref, g_ref):
    half = x_ref.shape[0] // 2
    out = []
    for rows in (slice(0, half), slice(half, 2 * half)):
        xf = x_ref[rows, :]
        out.append((xf * _rms_scale(xf) * g_ref[...]).astype(BF16))
    return out


def _dot_halves(u_halves, w):
    return jnp.concatenate([jnp.dot(u, w, preferred_element_type=F32) for u in u_halves], axis=0)


def _qkv_kernel(x_ref, g_ref, w_ref, cos_ref, sin_lo_ref, sin_hi_ref,
                qa_ref, ka_ref, va_ref, qb_ref, kb_ref, vb_ref, kmean_ref, *, tm):
    u_halves = _normed_halves(x_ref, g_ref)
    cos, sin_lo, sin_hi = cos_ref[...], sin_lo_ref[...], sin_hi_ref[...]

    def section(j):
        y = _dot_halves(u_halves, w_ref[:, j * SECTION:(j + 1) * SECTION])
        return [y[:, h * HEAD_DIM:(h + 1) * HEAD_DIM] for h in range(N_HEADS)]

    def rope(yh):
        from_hi = pltpu.roll(yh, HEAD_DIM - ROT_DIM // 2, axis=1)
        from_lo = pltpu.roll(yh, ROT_DIM // 2, axis=1)
        return yh * cos + from_hi * sin_lo + from_lo * sin_hi

    for h, yh in enumerate(section(0)):
        qa_ref[h] = (rope(yh) * Q_SCALE).T.astype(BF16)
    for h, yh in enumerate(section(1)):
        kr = rope(yh)
        ka_ref[h] = kr.astype(BF16)
        kmean_ref[h] = jnp.mean(kr.reshape(tm // MOBA_BLOCK, MOBA_BLOCK, HEAD_DIM), axis=1)
    ones_tile = jnp.where(lax.broadcasted_iota(jnp.int32, (ONES_ROWS, tm), 0) == 0, 1.0, 0.0).astype(BF16)
    for h, yh in enumerate(section(2)):
        va_ref[h, :HEAD_DIM] = yh.T.astype(BF16)
        va_ref[h, HEAD_DIM:] = ones_tile
    for h, yh in enumerate(section(3)):
        qb_ref[h] = (rope(yh) * Q_SCALE).T.astype(BF16)
    for h, yh in enumerate(section(4)):
        kb_ref[h] = rope(yh).astype(BF16)
    for h, yh in enumerate(section(5)):
        vb_ref[h] = yh.T.astype(BF16)


def _qkv_proj(x2, g, w_all, cos_t, sin_lo, sin_hi, seq, tm=512):
    t = x2.shape[0]
    n_rows = t // tm
    pos_blocks = seq // tm
    row_major = jax.ShapeDtypeStruct((N_HEADS, t, HEAD_DIM), BF16)
    col_major = jax.ShapeDtypeStruct((N_HEADS, HEAD_DIM, t), BF16)
    row_spec = pl.BlockSpec((N_HEADS, tm, HEAD_DIM), lambda i: (0, i, 0))
    col_spec = pl.BlockSpec((N_HEADS, HEAD_DIM, tm), lambda i: (0, 0, i))
    tab_spec = pl.BlockSpec((tm, HEAD_DIM), lambda i: (i % pos_blocks, 0))
    return pl.pallas_call(
        functools.partial(_qkv_kernel, tm=tm),
        grid=(n_rows,),
        in_specs=[
            pl.BlockSpec((tm, D_MODEL), lambda i: (i, 0)),
            pl.BlockSpec((1, D_MODEL), lambda i: (0, 0)),
            pl.BlockSpec((D_MODEL, QKV_WIDTH), lambda i: (0, 0), pipeline_mode=pl.Buffered(1)),
            tab_spec, tab_spec, tab_spec,
        ],
        out_specs=[
            col_spec, row_spec, pl.BlockSpec((N_HEADS, HEAD_DIM + ONES_ROWS, tm), lambda i: (0, 0, i)),
            col_spec, row_spec, col_spec,
            pl.BlockSpec((None, N_HEADS, tm // MOBA_BLOCK, HEAD_DIM), lambda i: (i, 0, 0, 0)),
        ],
        out_shape=[
            col_major, row_major, jax.ShapeDtypeStruct((N_HEADS, HEAD_DIM + ONES_ROWS, t), BF16),
            col_major, row_major, col_major,
            jax.ShapeDtypeStruct((n_rows, N_HEADS, tm // MOBA_BLOCK, HEAD_DIM), F32),
        ],
        compiler_params=_params(1),
        name="qkv_proj",
    )(x2, g, w_all, cos_t, sin_lo, sin_hi)


def _gate_kernel(x_ref, g_ref, wa_ref, wb_ref, o_ref, *, tn):
    u_halves = _normed_halves(x_ref, g_ref)
    for n, w_ref in enumerate((wa_ref, wb_ref)):
        for j in range(D_MODEL // tn):
            y = _dot_halves(u_halves, w_ref[:, j * tn:(j + 1) * tn])
            o_ref[:, n * D_MODEL + j * tn:n * D_MODEL + (j + 1) * tn] = jax.nn.sigmoid(y).astype(o_ref.dtype)


def _gate_proj(x2, g, w_all, tm=512, tn=1024):
    t = x2.shape[0]
    first = QKV_WIDTH // D_MODEL
    gate_w = lambda n: pl.BlockSpec((D_MODEL, D_MODEL), lambda i: (0, first + n), pipeline_mode=pl.Buffered(1))
    return pl.pallas_call(
        functools.partial(_gate_kernel, tn=tn),
        grid=(t // tm,),
        in_specs=[
            pl.BlockSpec((tm, D_MODEL), lambda i: (i, 0)),
            pl.BlockSpec((1, D_MODEL), lambda i: (0, 0)),
            gate_w(0), gate_w(1),
        ],
        out_specs=pl.BlockSpec((tm, GATE_WIDTH), lambda i: (i, 0)),
        out_shape=jax.ShapeDtypeStruct((t, GATE_WIDTH), BF16),
        compiler_params=_params(1),
        name="gate_proj",
    )(x2, g, w_all, w_all)


def _past_chunks_loop(n_chunks, body, carry):
    done = 0
    for width in LOOP_UNROLLS:
        def group(i, carry, width=width, base=done):
            for u in range(width):
                carry = body(base + width * i + u, carry)
            return carry

        trips = (n_chunks - done) // width
        carry = lax.fori_loop(0, trips, group, carry)
        done = done + trips * width
    return carry


def _causal_bias():
    key = lax.broadcasted_iota(jnp.int32, (ATTN_TQ, ATTN_TQ), 0)
    qry = lax.broadcasted_iota(jnp.int32, (ATTN_TQ, ATTN_TQ), 1)
    return jnp.where(key <= qry, 0.0, NEG_INF)


def _moba_kernel(q_ref, k_ref, v_ref, kmean_ref, o_ref, acc_ref, s0_ref, s1_ref, s2_ref, s3_ref, *, n_blocks):
    sup = pl.program_id(2)
    q_all = q_ref[...]

    km = kmean_ref[...]
    km_hi = km.astype(BF16)
    km_mid = (km - km_hi.astype(F32)).astype(BF16)
    km_lo = (km - km_hi.astype(F32) - km_mid.astype(F32)).astype(BF16)
    gate = (jnp.dot(km_lo, q_all, preferred_element_type=F32) + jnp.dot(km_mid, q_all, preferred_element_type=F32)
            + jnp.dot(km_hi, q_all, preferred_element_type=F32))
    blk = lax.broadcasted_iota(jnp.int32, gate.shape, 0).astype(F32)
    own_blk = (sup * TILES_PER_STEP
               + lax.broadcasted_iota(jnp.int32, (1, ATTN_CHUNK), 1) // MOBA_BLOCK).astype(F32)
    g = jnp.where(blk < own_blk, gate, NEG_INF)
    picks = []
    for _ in range(MOBA_TOPK):
        best = jnp.max(g, axis=0, keepdims=True)
        first = jnp.min(jnp.where(g == best, blk, float(n_blocks)), axis=0, keepdims=True)
        first = jnp.where(best > NEG_INF, first, -1.0)
        picks.append(first)
        g = jnp.where(blk == first, NEG_INF, g)

    def lanes(a):
        return slice(a * ATTN_TQ, (a + 1) * ATTN_TQ)

    q_t = [q_all[:, lanes(a)] for a in range(TILES_PER_STEP)]
    tile_picks = [[p[:, lanes(a)] for p in picks] for a in range(TILES_PER_STEP)]
    s_refs = (s0_ref, s1_ref, s2_ref, s3_ref)
    causal_bias = _causal_bias()

    def selection_bias(a, block_index):
        jf = jnp.asarray(block_index).astype(F32)
        chosen = (tile_picks[a][0] == jf) | (tile_picks[a][1] == jf) | (tile_picks[a][2] == jf)
        return jnp.where(chosen, 0.0, NEG_INF)

    def past_biases(a, c):
        return [selection_bias(a, c * BLOCKS_PER_CHUNK + r) for r in range(BLOCKS_PER_CHUNK)]

    def own_biases(a):
        return [selection_bias(a, sup * TILES_PER_STEP + r) for r in range(a)] + [causal_bias]

    def scores_pass(a, start, biases):
        top = None
        for r, bias in enumerate(biases):
            k_r = k_ref[pl.ds(start + r * MOBA_BLOCK, MOBA_BLOCK), :]
            s_r = jnp.dot(k_r, q_t[a], preferred_element_type=F32)
            if bias.shape[0] != 1:
                s_r = s_r + bias
            s_refs[a][r * MOBA_BLOCK:(r + 1) * MOBA_BLOCK, :] = s_r
            part = jnp.max(s_r.reshape(MOBA_BLOCK // SUBLANES, SUBLANES, ATTN_TQ), axis=0)
            if bias.shape[0] == 1:
                part = part + bias
            top = part if top is None else jnp.maximum(top, part)
        return jnp.max(top, axis=0, keepdims=True)

    def values_pass(a, start, biases, m_new, alpha):
        p_blocks = []
        for r, bias in enumerate(biases):
            shift = m_new - bias if bias.shape[0] == 1 else m_new
            p_blocks.append(jnp.exp2(s_refs[a][r * MOBA_BLOCK:(r + 1) * MOBA_BLOCK, :] - shift).astype(BF16))
        rows = len(biases) * MOBA_BLOCK
        update = jnp.dot(v_ref[:, pl.ds(start, rows)], jnp.concatenate(p_blocks, axis=0),
                         preferred_element_type=F32)
        acc_ref[a] = update if alpha is None else alpha * acc_ref[a] + update

    own = pl.multiple_of(sup * ATTN_CHUNK, ATTN_CHUNK)
    m_chunk = scores_pass(0, own, own_biases(0))
    tops = []
    for a in range(TILES_PER_STEP):
        if a + 1 < TILES_PER_STEP:
            m_next = scores_pass(a + 1, own, own_biases(a + 1))
        else:
            m_next = scores_pass(0, 0, past_biases(0, 0))
        values_pass(a, own, own_biases(a), m_chunk, None)
        tops.append(m_chunk)
        m_chunk = m_next

    def body(c, carry):
        tops, m_chunk = carry
        start = pl.multiple_of(c * ATTN_CHUNK, ATTN_CHUNK)
        out = []
        for a in range(TILES_PER_STEP):
            if a + 1 < TILES_PER_STEP:
                m_next = scores_pass(a + 1, start, past_biases(a + 1, c))
            else:
                m_next = scores_pass(0, pl.multiple_of((c + 1) * ATTN_CHUNK, ATTN_CHUNK), past_biases(0, c + 1))
            m_new = jnp.maximum(tops[a], m_chunk)
            values_pass(a, start, past_biases(a, c), m_new, jnp.exp2(tops[a] - m_new))
            out.append(m_new)
            m_chunk = m_next
        return tuple(out), m_chunk

    _past_chunks_loop(sup, body, (tuple(tops), m_chunk))
    for a in range(TILES_PER_STEP):
        acc = acc_ref[a]
        o_ref[lanes(a), :] = (acc[:HEAD_DIM] * (1.0 / acc[HEAD_DIM:HEAD_DIM + 1])).T.astype(o_ref.dtype)


def _moba_attention(qa_t, ka, va_t, kmean, batch, seq):
    t = batch * seq
    n_blocks = seq // MOBA_BLOCK
    n_steps = seq // ATTN_CHUNK
    return pl.pallas_call(
        functools.partial(_moba_kernel, n_blocks=n_blocks),
        grid=(batch, N_HEADS, n_steps),
        in_specs=[
            pl.BlockSpec((None, HEAD_DIM, ATTN_CHUNK), lambda b, h, i: (h, 0, b * n_steps + i)),
            pl.BlockSpec((None, seq, HEAD_DIM), lambda b, h, i: (h, b, 0)),
            pl.BlockSpec((None, HEAD_DIM + ONES_ROWS, seq), lambda b, h, i: (h, 0, b)),
            pl.BlockSpec((None, n_blocks, HEAD_DIM), lambda b, h, i: (h, b, 0)),
        ],
        out_specs=pl.BlockSpec((ATTN_CHUNK, HEAD_DIM), lambda b, h, i: (b * n_steps + i, h)),
        out_shape=jax.ShapeDtypeStruct((t, SECTION), BF16),
        scratch_shapes=[pltpu.VMEM((TILES_PER_STEP, HEAD_DIM + ONES_ROWS, ATTN_TQ), F32)]
                       + [pltpu.VMEM((ATTN_CHUNK, ATTN_TQ), F32)] * TILES_PER_STEP,
        compiler_params=_params(3),
        name="moba_attn",
    )(qa_t, ka, va_t, kmean)


def _diff_kernel(q0_ref, q1_ref, k0_ref, k1_ref, vlo_ref, vhi_ref, lq1_ref, lk1_ref, lq2_ref, lk2_ref, subg_ref,
                 o_ref, acc_ref, *s_refs, lambda_init):
    sup = pl.program_id(2)
    q_refs = (q0_ref, q1_ref)
    k_refs = (k0_ref, k1_ref)
    chains = [(a, sub) for a in range(TILES_PER_STEP) for sub in range(2)]
    n_chains = len(chains)
    q_t = [q_refs[sub][:, a * ATTN_TQ:(a + 1) * ATTN_TQ] for a, sub in chains]
    causal_bias = _causal_bias()

    def scores_pass(n, start, n_blocks, causal_last):
        top = None
        for r in range(n_blocks):
            k_r = k_refs[chains[n][1]][pl.ds(start + r * ATTN_TQ, ATTN_TQ), :]
            s_r = jnp.dot(k_r, q_t[n], preferred_element_type=F32)
            if causal_last and r == n_blocks - 1:
                s_r = s_r + causal_bias
            s_refs[n][r * ATTN_TQ:(r + 1) * ATTN_TQ, :] = s_r
            part = jnp.max(s_r.reshape(ATTN_TQ // SUBLANES, SUBLANES, ATTN_TQ), axis=0)
            top = part if top is None else jnp.maximum(top, part)
        return jnp.max(top, axis=0, keepdims=True)

    def values_pass(n, start, rows, m_new, alpha, l):
        p_t = jnp.exp2(s_refs[n][:rows, :] - m_new)
        l_chunk = jnp.sum(jnp.sum(p_t.reshape(rows // SUBLANES, SUBLANES, ATTN_TQ), axis=0), axis=0, keepdims=True)
        v_t = jnp.concatenate([vlo_ref[:, pl.ds(start, rows)], vhi_ref[:, pl.ds(start, rows)]], axis=0)
        update = jnp.dot(v_t, p_t.astype(BF16), preferred_element_type=F32)
        if alpha is None:
            acc_ref[n] = update
            return l_chunk
        acc_ref[n] = alpha * acc_ref[n] + update
        return alpha * l + l_chunk

    own = pl.multiple_of(sup * ATTN_CHUNK, ATTN_CHUNK)
    m_chunk = scores_pass(0, own, 1, True)
    stats = []
    for n, (a, sub) in enumerate(chains):
        if n + 1 < n_chains:
            m_next = scores_pass(n + 1, own, chains[n + 1][0] + 1, True)
        else:
            m_next = scores_pass(0, 0, BLOCKS_PER_CHUNK, False)
        stats.append((m_chunk, values_pass(n, own, (a + 1) * ATTN_TQ, m_chunk, None, None)))
        m_chunk = m_next

    def body(c, carry):
        stats, m_chunk = carry
        start = pl.multiple_of(c * ATTN_CHUNK, ATTN_CHUNK)
        out = []
        for n in range(n_chains):
            if n + 1 < n_chains:
                m_next = scores_pass(n + 1, start, BLOCKS_PER_CHUNK, False)
            else:
                m_next = scores_pass(0, pl.multiple_of((c + 1) * ATTN_CHUNK, ATTN_CHUNK), BLOCKS_PER_CHUNK, False)
            m, l = stats[n]
            m_new = jnp.maximum(m, m_chunk)
            out.append((m_new, values_pass(n, start, ATTN_CHUNK, m_new, jnp.exp2(m - m_new), l)))
            m_chunk = m_next
        return tuple(out), m_chunk

    stats, _ = _past_chunks_loop(sup, body, (tuple(stats), m_chunk))

    lam = (jnp.exp(jnp.sum(lq1_ref[...] * lk1_ref[...], axis=1, keepdims=True))
           - jnp.exp(jnp.sum(lq2_ref[...] * lk2_ref[...], axis=1, keepdims=True)) + lambda_init)
    for a in range(TILES_PER_STEP):
        out = (acc_ref[2 * a] * (1.0 / stats[2 * a][1])
               - acc_ref[2 * a + 1] * (lam / stats[2 * a + 1][1]))
        inv = lax.rsqrt(jnp.mean(out * out, axis=0, keepdims=True) + NORM_EPS)
        out = out * inv * subg_ref[...]
        o_ref[a * ATTN_TQ:(a + 1) * ATTN_TQ, :] = (out * (1.0 - lambda_init)).T.astype(o_ref.dtype)


def _diff_attention(qb_t, kb, vb_t, lq1, lk1, lq2, lk2, sub_g_col, batch, seq, lambda_init):
    t = batch * seq
    n_steps = seq // ATTN_CHUNK
    q_spec = lambda c: pl.BlockSpec((None, HEAD_DIM, ATTN_CHUNK), lambda b, h, i: (2 * h + c, 0, b * n_steps + i))
    k_spec = lambda c: pl.BlockSpec((None, seq, HEAD_DIM), lambda b, h, i: (2 * h + c, b, 0))
    v_spec = lambda c: pl.BlockSpec((None, HEAD_DIM, seq), lambda b, h, i: (2 * h + c, 0, b))
    vec_spec = pl.BlockSpec((1, HEAD_DIM), lambda b, h, i: (0, 0))
    return pl.pallas_call(
        functools.partial(_diff_kernel, lambda_init=lambda_init),
        grid=(batch, DIFF_HEADS, n_steps),
        in_specs=[
            q_spec(0), q_spec(1), k_spec(0), k_spec(1), v_spec(0), v_spec(1),
            vec_spec, vec_spec, vec_spec, vec_spec,
            pl.BlockSpec((DIFF_V_DIM, 1), lambda b, h, i: (0, 0)),
        ],
        out_specs=pl.BlockSpec((ATTN_CHUNK, DIFF_V_DIM), lambda b, h, i: (b * n_steps + i, h)),
        out_shape=jax.ShapeDtypeStruct((t, DIFF_HEADS * DIFF_V_DIM), BF16),
        scratch_shapes=[pltpu.VMEM((2 * TILES_PER_STEP, DIFF_V_DIM, ATTN_TQ), F32)]
                       + [pltpu.VMEM((ATTN_CHUNK, ATTN_TQ), F32)] * (2 * TILES_PER_STEP),
        compiler_params=_params(3),
        name="diff_attn",
    )(qb_t, qb_t, kb, kb, vb_t, vb_t, lq1, lk1, lq2, lk2, sub_g_col)


def _mix_kernel(oa_ref, ob_ref, sga_ref, sgb_ref, x_ref, wm_ref, wd_ref, wo_ref, gpost_ref, gnext_ref,
                h_ref, u_ref):
    half = x_ref.shape[0] // 2
    for rows in (slice(0, half), slice(half, 2 * half)):
        ya = jnp.dot(oa_ref[rows, :], wm_ref[...], preferred_element_type=F32)
        yb = jnp.dot(ob_ref[rows, :], wd_ref[...], preferred_element_type=F32)
        mixed = sga_ref[rows, :].astype(F32) * ya + sgb_ref[rows, :].astype(F32) * yb
        z = jnp.dot(mixed.astype(BF16), wo_ref[...], preferred_element_type=F32)
        h = x_ref[rows, :] + z * _rms_scale(z) * gpost_ref[...]
        h_ref[rows, :] = h
        u_ref[rows, :] = (h * _rms_scale(h) * gnext_ref[...]).astype(BF16)


def _mix(oa, ob, gates, x2, wm, wd, wo, gpost, gnext, tm=512):
    t = x2.shape[0]
    const = lambda shape: pl.BlockSpec(shape, lambda i: (0, 0), pipeline_mode=pl.Buffered(1))
    row = lambda width, col=0: pl.BlockSpec((tm, width), lambda i: (i, col))
    return pl.pallas_call(
        _mix_kernel,
        grid=(t // tm,),
        in_specs=[
            row(SECTION), row(SECTION), row(D_MODEL, 0), row(D_MODEL, 1), row(D_MODEL),
            const((SECTION, D_MODEL)), const((SECTION, D_MODEL)), const((D_MODEL, D_MODEL)),
            const((1, D_MODEL)), const((1, D_MODEL)),
        ],
        out_specs=[row(D_MODEL), row(D_MODEL)],
        out_shape=[jax.ShapeDtypeStruct((t, D_MODEL), F32), jax.ShapeDtypeStruct((t, D_MODEL), BF16)],
        compiler_params=_params(1),
        name="mix_out",
    )(oa, ob, gates, gates, x2, wm, wd, wo, gpost, gnext)


def _mlp_kernel(u_ref, wup_ref, wdown_ref, o_ref):
    @pl.when(pl.program_id(1) == 0)
    def _():
        o_ref[...] = jnp.zeros_like(o_ref)

    a = jnp.dot(u_ref[...], wup_ref[...], preferred_element_type=F32)
    a = jnp.square(jnp.maximum(a, 0.0)).astype(BF16)
    o_ref[...] += jnp.dot(a, wdown_ref[...], preferred_element_type=F32)


def _mlp(u, wup, wdown, tm=1024, tf=1024):
    t = u.shape[0]
    return pl.pallas_call(
        _mlp_kernel,
        grid=(t // tm, D_FF // tf),
        in_specs=[
            pl.BlockSpec((tm, D_MODEL), lambda i, k: (i, 0)),
            pl.BlockSpec((D_MODEL, tf), lambda i, k: (0, k)),
            pl.BlockSpec((tf, D_MODEL), lambda i, k: (k, 0)),
        ],
        out_specs=pl.BlockSpec((tm, D_MODEL), lambda i, k: (i, 0)),
        out_shape=jax.ShapeDtypeStruct((t, D_MODEL), F32),
        compiler_params=_params(2),
        name="mlp",
    )(u, wup, wdown)


def _ple_kernel(h_ref, ff_ref, p_ref, gmlp_ref, gpre_ref, wgate_ref, wproj_ref, gpost_ref, o_ref):
    half = h_ref.shape[0] // 2
    halves = [slice(i * half, (i + 1) * half) for i in range(2)]
    normed = []
    for rows in halves:
        ff = ff_ref[rows, :]
        h = h_ref[rows, :] + ff * _rms_scale(ff) * gmlp_ref[...]
        normed.append((h, (h * _rms_scale(h) * gpre_ref[...]).astype(BF16)))
    for rows, (h, u) in zip(halves, normed):
        gate = jax.nn.sigmoid(jnp.dot(u, wgate_ref[...], preferred_element_type=F32))
        e = jnp.dot(p_ref[rows, :].astype(BF16), wproj_ref[...], preferred_element_type=F32) * gate
        o_ref[rows, :] = h + e * _rms_scale(e) * gpost_ref[...]


def _ple(h, ff, p2, gmlp, gpre, wgate, wproj, gpost, tm=512):
    t = h.shape[0]
    const = lambda shape: pl.BlockSpec(shape, lambda i: (0, 0), pipeline_mode=pl.Buffered(1))
    row = lambda width: pl.BlockSpec((tm, width), lambda i: (i, 0))
    return pl.pallas_call(
        _ple_kernel,
        grid=(t // tm,),
        in_specs=[
            row(D_MODEL), row(D_MODEL), row(PLE_DIM),
            const((1, D_MODEL)), const((1, D_MODEL)), const((D_MODEL, D_MODEL)), const((PLE_DIM, D_MODEL)),
            const((1, D_MODEL)),
        ],
        out_specs=row(D_MODEL),
        out_shape=jax.ShapeDtypeStruct((t, D_MODEL), F32),
        compiler_params=_params(1),
        name="ple",
    )(h, ff, p2, gmlp, gpre, wgate, wproj, gpost)


def _rope_tables(seq):
    half = ROT_DIM // 2
    inv_freq = 1.0 / (ROPE_THETA ** (np.arange(half, dtype=np.float64) * 2.0 / ROT_DIM))
    ang = np.arange(seq, dtype=np.float64)[:, None] * inv_freq[None, :]
    cos, sin = np.cos(ang), np.sin(ang)
    zeros = np.zeros((seq, half))
    rest = np.zeros((seq, HEAD_DIM - ROT_DIM))
    cos_t = np.concatenate([cos, cos, rest + 1.0], axis=1)
    sin_lo = np.concatenate([-sin, zeros, rest], axis=1)
    sin_hi = np.concatenate([zeros, sin, rest], axis=1)
    return tuple(jnp.asarray(t.astype(np.float32)) for t in (cos_t, sin_lo, sin_hi))


def kernel(x, p, w_in, w_br_moba, w_br_diff, w_out, lambda_q1, lambda_k1, lambda_q2, lambda_k2, diff_subln_g,
           g_mix_pre, g_mix_post, w_up, w_down, g_mlp_pre, g_mlp_post, w_ple_proj, w_ple_gate, g_ple_pre,
           g_ple_post):
    batch, seq, _ = x.shape
    depth = w_in.shape[0]
    t = batch * seq
    n_blocks = seq // MOBA_BLOCK
    cos_t, sin_lo, sin_hi = _rope_tables(seq)
    row = lambda v: v.reshape(1, -1).astype(F32)

    h = x.reshape(t, D_MODEL)
    for layer in range(depth):
        lambda_init = 0.8 - 0.6 * math.exp(-0.3 * layer)
        w_l = w_in[layer].astype(BF16)
        qa_t, ka, va_t, qb_t, kb, vb_t, kmean = _qkv_proj(h, row(g_mix_pre[layer]), w_l, cos_t, sin_lo, sin_hi, seq)
        gates = _gate_proj(h, row(g_mix_pre[layer]), w_l)
        kmean = kmean.transpose(1, 0, 2, 3).reshape(N_HEADS, batch * n_blocks, HEAD_DIM)
        oa = _moba_attention(qa_t, ka, va_t, kmean, batch, seq)
        ob = _diff_attention(qb_t, kb, vb_t, row(lambda_q1[layer]), row(lambda_k1[layer]), row(lambda_q2[layer]),
                             row(lambda_k2[layer]), diff_subln_g[layer].reshape(-1, 1).astype(F32),
                             batch, seq, lambda_init)
        h, u = _mix(oa, ob, gates, h, w_br_moba[layer].astype(BF16), w_br_diff[layer].astype(BF16),
                    w_out[layer].astype(BF16), row(g_mix_post[layer]), row(g_mlp_pre[layer]))
        ff = _mlp(u, w_up[layer].astype(BF16), w_down[layer].astype(BF16))
        h = _ple(h, ff, p[layer].reshape(t, PLE_DIM), row(g_mlp_post[layer]), row(g_ple_pre[layer]),
                 w_ple_gate[layer].astype(BF16), w_ple_proj[layer].astype(BF16), row(g_ple_post[layer]))
    return h.reshape(batch, seq, D_MODEL)
```

```python
import functools
import math

import jax
import jax.numpy as jnp
import numpy as np
from jax import lax
from jax.experimental import pallas as pl
from jax.experimental.pallas import tpu as pltpu

F32 = jnp.float32
BF16 = jnp.bfloat16

D_MODEL = 2048
HEAD_DIM = 128
N_HEADS = 8
MOBA_BLOCK = 256
MOBA_TOPK = 3
DIFF_HEADS = 4
DIFF_V_DIM = 2 * HEAD_DIM
ROT_DIM = HEAD_DIM // 4
ROPE_THETA = 500000.0
D_FF = 4 * D_MODEL
PLE_DIM = 256
NORM_EPS = 1e-6
SECTION = N_HEADS * HEAD_DIM
QKV_WIDTH = 6 * SECTION
GATE_WIDTH = 2 * D_MODEL

ATTN_TQ = 256
ATTN_CHUNK = 1024
BLOCKS_PER_CHUNK = ATTN_CHUNK // MOBA_BLOCK
TILES_PER_STEP = ATTN_CHUNK // ATTN_TQ
SUBLANES = 8
ONES_ROWS = 2 * SUBLANES
LOOP_UNROLLS = (4, 2, 1)

VMEM_LIMIT_BYTES = 56 * 1024 * 1024

NEG_INF = float("-inf")
Q_SCALE = HEAD_DIM ** -0.5 * math.log2(math.e)


def _params(n_axes):
    return pltpu.CompilerParams(dimension_semantics=("arbitrary",) * n_axes,
                                vmem_limit_bytes=VMEM_LIMIT_BYTES)


def _rms_scale(xf):
    return lax.rsqrt(jnp.mean(xf * xf, axis=-1, keepdims=True) + NORM_EPS)


def _normed_halves(x_ref, g_ref):
    half = x_ref.shape[0] // 2
    out = []
    for rows in (slice(0, half), slice(half, 2 * half)):
        xf = x_ref[rows, :]
        out.append((xf * _rms_scale(xf) * g_ref[...]).astype(BF16))
    return out


def _dot_halves(u_halves, w):
    return jnp.concatenate([jnp.dot(u, w, preferred_element_type=F32) for u in u_halves], axis=0)


def _qkv_kernel(x_ref, g_ref, w_ref, cos_ref, sin_lo_ref, sin_hi_ref,
                qa_ref, ka_ref, va_ref, qb_ref, kb_ref, vb_ref, kmean_ref, *, tm):
    u_halves = _normed_halves(x_ref, g_ref)
    cos, sin_lo, sin_hi = cos_ref[...], sin_lo_ref[...], sin_hi_ref[...]

    def section(j):
        y = _dot_halves(u_halves, w_ref[:, j * SECTION:(j + 1) * SECTION])
        return [y[:, h * HEAD_DIM:(h + 1) * HEAD_DIM] for h in range(N_HEADS)]

    def rope(yh):
        from_hi = pltpu.roll(yh, HEAD_DIM - ROT_DIM // 2, axis=1)
        from_lo = pltpu.roll(yh, ROT_DIM // 2, axis=1)
        return yh * cos + from_hi * sin_lo + from_lo * sin_hi

    for h, yh in enumerate(section(0)):
        qa_ref[h] = (rope(yh) * Q_SCALE).T.astype(BF16)
    for h, yh in enumerate(section(1)):
        kr = rope(yh)
        ka_ref[h] = kr.astype(BF16)
        kmean_ref[h] = jnp.mean(kr.reshape(tm // MOBA_BLOCK, MOBA_BLOCK, HEAD_DIM), axis=1)
    ones_tile = jnp.where(lax.broadcasted_iota(jnp.int32, (ONES_ROWS, tm), 0) == 0, 1.0, 0.0).astype(BF16)
    for h, yh in enumerate(section(2)):
        va_ref[h, :HEAD_DIM] = yh.T.astype(BF16)
        va_ref[h, HEAD_DIM:] = ones_tile
    for h, yh in enumerate(section(3)):
        qb_ref[h] = (rope(yh) * Q_SCALE).T.astype(BF16)
    for h, yh in enumerate(section(4)):
        kb_ref[h] = rope(yh).astype(BF16)
    for h, yh in enumerate(section(5)):
        vb_ref[h, :HEAD_DIM] = yh.T.astype(BF16)
        vb_ref[h, HEAD_DIM:] = ones_tile


def _qkv_proj(x2, g, w_all, cos_t, sin_lo, sin_hi, seq, tm=512):
    t = x2.shape[0]
    n_rows = t // tm
    pos_blocks = seq // tm
    row_major = jax.ShapeDtypeStruct((N_HEADS, t, HEAD_DIM), BF16)
    col_major = jax.ShapeDtypeStruct((N_HEADS, HEAD_DIM, t), BF16)
    row_spec = pl.BlockSpec((N_HEADS, tm, HEAD_DIM), lambda i: (0, i, 0))
    col_spec = pl.BlockSpec((N_HEADS, HEAD_DIM, tm), lambda i: (0, 0, i))
    tab_spec = pl.BlockSpec((tm, HEAD_DIM), lambda i: (i % pos_blocks, 0))
    val_major = jax.ShapeDtypeStruct((N_HEADS, HEAD_DIM + ONES_ROWS, t), BF16)
    val_spec = pl.BlockSpec((N_HEADS, HEAD_DIM + ONES_ROWS, tm), lambda i: (0, 0, i))
    return pl.pallas_call(
        functools.partial(_qkv_kernel, tm=tm),
        grid=(n_rows,),
        in_specs=[
            pl.BlockSpec((tm, D_MODEL), lambda i: (i, 0)),
            pl.BlockSpec((1, D_MODEL), lambda i: (0, 0)),
            pl.BlockSpec((D_MODEL, QKV_WIDTH), lambda i: (0, 0), pipeline_mode=pl.Buffered(1)),
            tab_spec, tab_spec, tab_spec,
        ],
        out_specs=[
            col_spec, row_spec, val_spec,
            col_spec, row_spec, val_spec,
            pl.BlockSpec((None, N_HEADS, tm // MOBA_BLOCK, HEAD_DIM), lambda i: (i, 0, 0, 0)),
        ],
        out_shape=[
            col_major, row_major, val_major,
            col_major, row_major, val_major,
            jax.ShapeDtypeStruct((n_rows, N_HEADS, tm // MOBA_BLOCK, HEAD_DIM), F32),
        ],
        compiler_params=_params(1),
        name="qkv_proj",
    )(x2, g, w_all, cos_t, sin_lo, sin_hi)


def _gate_kernel(x_ref, g_ref, wa_ref, wb_ref, o_ref, *, tn):
    u_halves = _normed_halves(x_ref, g_ref)
    for n, w_ref in enumerate((wa_ref, wb_ref)):
        for j in range(D_MODEL // tn):
            y = _dot_halves(u_halves, w_ref[:, j * tn:(j + 1) * tn])
            o_ref[:, n * D_MODEL + j * tn:n * D_MODEL + (j + 1) * tn] = jax.nn.sigmoid(y).astype(o_ref.dtype)


def _gate_proj(x2, g, w_all, tm=512, tn=1024):
    t = x2.shape[0]
    first = QKV_WIDTH // D_MODEL
    gate_w = lambda n: pl.BlockSpec((D_MODEL, D_MODEL), lambda i: (0, first + n), pipeline_mode=pl.Buffered(1))
    return pl.pallas_call(
        functools.partial(_gate_kernel, tn=tn),
        grid=(t // tm,),
        in_specs=[
            pl.BlockSpec((tm, D_MODEL), lambda i: (i, 0)),
            pl.BlockSpec((1, D_MODEL), lambda i: (0, 0)),
            gate_w(0), gate_w(1),
        ],
        out_specs=pl.BlockSpec((tm, GATE_WIDTH), lambda i: (i, 0)),
        out_shape=jax.ShapeDtypeStruct((t, GATE_WIDTH), BF16),
        compiler_params=_params(1),
        name="gate_proj",
    )(x2, g, w_all, w_all)


def _past_chunks_loop(n_chunks, body, carry):
    done = 0
    for width in LOOP_UNROLLS:
        def group(i, carry, width=width, base=done):
            for u in range(width):
                carry = body(base + width * i + u, carry)
            return carry

        trips = (n_chunks - done) // width
        carry = lax.fori_loop(0, trips, group, carry)
        done = done + trips * width
    return carry


def _causal_bias():
    key = lax.broadcasted_iota(jnp.int32, (ATTN_TQ, ATTN_TQ), 0)
    qry = lax.broadcasted_iota(jnp.int32, (ATTN_TQ, ATTN_TQ), 1)
    return jnp.where(key <= qry, 0.0, NEG_INF)


def _moba_kernel(q_ref, k_ref, v_ref, kmean_ref, o_ref, acc_ref, s0_ref, s1_ref, s2_ref, s3_ref, *, n_blocks):
    sup = pl.program_id(2)
    q_all = q_ref[...]

    km = kmean_ref[...]
    km_hi = km.astype(BF16)
    km_mid = (km - km_hi.astype(F32)).astype(BF16)
    km_lo = (km - km_hi.astype(F32) - km_mid.astype(F32)).astype(BF16)
    gate = (jnp.dot(km_lo, q_all, preferred_element_type=F32) + jnp.dot(km_mid, q_all, preferred_element_type=F32)
            + jnp.dot(km_hi, q_all, preferred_element_type=F32))
    blk = lax.broadcasted_iota(jnp.int32, gate.shape, 0).astype(F32)
    own_blk = (sup * TILES_PER_STEP
               + lax.broadcasted_iota(jnp.int32, (1, ATTN_CHUNK), 1) // MOBA_BLOCK).astype(F32)
    g = jnp.where(blk < own_blk, gate, NEG_INF)
    picks = []
    for _ in range(MOBA_TOPK):
        best = jnp.max(g, axis=0, keepdims=True)
        first = jnp.min(jnp.where(g == best, blk, float(n_blocks)), axis=0, keepdims=True)
        first = jnp.where(best > NEG_INF, first, -1.0)
        picks.append(first)
        g = jnp.where(blk == first, NEG_INF, g)

    def lanes(a):
        return slice(a * ATTN_TQ, (a + 1) * ATTN_TQ)

    q_t = [q_all[:, lanes(a)] for a in range(TILES_PER_STEP)]
    tile_picks = [[p[:, lanes(a)] for p in picks] for a in range(TILES_PER_STEP)]
    s_refs = (s0_ref, s1_ref, s2_ref, s3_ref)
    causal_bias = _causal_bias()

    def selection_bias(a, block_index):
        jf = jnp.asarray(block_index).astype(F32)
        chosen = (tile_picks[a][0] == jf) | (tile_picks[a][1] == jf) | (tile_picks[a][2] == jf)
        return jnp.where(chosen, 0.0, NEG_INF)

    def past_biases(a, c):
        return [selection_bias(a, c * BLOCKS_PER_CHUNK + r) for r in range(BLOCKS_PER_CHUNK)]

    def own_biases(a):
        return [selection_bias(a, sup * TILES_PER_STEP + r) for r in range(a)] + [causal_bias]

    def scores_pass(a, start, biases):
        top = None
        for r, bias in enumerate(biases):
            k_r = k_ref[pl.ds(start + r * MOBA_BLOCK, MOBA_BLOCK), :]
            s_r = jnp.dot(k_r, q_t[a], preferred_element_type=F32)
            if bias.shape[0] != 1:
                s_r = s_r + bias
            s_refs[a][r * MOBA_BLOCK:(r + 1) * MOBA_BLOCK, :] = s_r
            part = jnp.max(s_r.reshape(MOBA_BLOCK // SUBLANES, SUBLANES, ATTN_TQ), axis=0)
            if bias.shape[0] == 1:
                part = part + bias
            top = part if top is None else jnp.maximum(top, part)
        return jnp.max(top, axis=0, keepdims=True)

    def values_pass(a, start, biases, m_new, alpha):
        p_blocks = []
        for r, bias in enumerate(biases):
            shift = m_new - bias if bias.shape[0] == 1 else m_new
            p_blocks.append(jnp.exp2(s_refs[a][r * MOBA_BLOCK:(r + 1) * MOBA_BLOCK, :] - shift).astype(BF16))
        rows = len(biases) * MOBA_BLOCK
        update = jnp.dot(v_ref[:, pl.ds(start, rows)], jnp.concatenate(p_blocks, axis=0),
                         preferred_element_type=F32)
        acc_ref[a] = update if alpha is None else alpha * acc_ref[a] + update

    own = pl.multiple_of(sup * ATTN_CHUNK, ATTN_CHUNK)
    m_chunk = scores_pass(0, own, own_biases(0))
    tops = []
    for a in range(TILES_PER_STEP):
        if a + 1 < TILES_PER_STEP:
            m_next = scores_pass(a + 1, own, own_biases(a + 1))
        else:
            m_next = scores_pass(0, 0, past_biases(0, 0))
        values_pass(a, own, own_biases(a), m_chunk, None)
        tops.append(m_chunk)
        m_chunk = m_next

    def body(c, carry):
        tops, m_chunk = carry
        start = pl.multiple_of(c * ATTN_CHUNK, ATTN_CHUNK)
        out = []
        for a in range(TILES_PER_STEP):
            if a + 1 < TILES_PER_STEP:
                m_next = scores_pass(a + 1, start, past_biases(a + 1, c))
            else:
                m_next = scores_pass(0, pl.multiple_of((c + 1) * ATTN_CHUNK, ATTN_CHUNK), past_biases(0, c + 1))
            m_new = jnp.maximum(tops[a], m_chunk)
            values_pass(a, start, past_biases(a, c), m_new, jnp.exp2(tops[a] - m_new))
            out.append(m_new)
            m_chunk = m_next
        return tuple(out), m_chunk

    _past_chunks_loop(sup, body, (tuple(tops), m_chunk))
    for a in range(TILES_PER_STEP):
        acc = acc_ref[a]
        o_ref[lanes(a), :] = (acc[:HEAD_DIM] * (1.0 / acc[HEAD_DIM:HEAD_DIM + 1])).T.astype(o_ref.dtype)


def _moba_attention(qa_t, ka, va_t, kmean, batch, seq):
    t = batch * seq
    n_blocks = seq // MOBA_BLOCK
    n_steps = seq // ATTN_CHUNK
    return pl.pallas_call(
        functools.partial(_moba_kernel, n_blocks=n_blocks),
        grid=(batch, N_HEADS, n_steps),
        in_specs=[
            pl.BlockSpec((None, HEAD_DIM, ATTN_CHUNK), lambda b, h, i: (h, 0, b * n_steps + i)),
            pl.BlockSpec((None, seq, HEAD_DIM), lambda b, h, i: (h, b, 0)),
            pl.BlockSpec((None, HEAD_DIM + ONES_ROWS, seq), lambda b, h, i: (h, 0, b)),
            pl.BlockSpec((None, n_blocks, HEAD_DIM), lambda b, h, i: (h, b, 0)),
        ],
        out_specs=pl.BlockSpec((ATTN_CHUNK, HEAD_DIM), lambda b, h, i: (b * n_steps + i, h)),
        out_shape=jax.ShapeDtypeStruct((t, SECTION), BF16),
        scratch_shapes=[pltpu.VMEM((TILES_PER_STEP, HEAD_DIM + ONES_ROWS, ATTN_TQ), F32)]
                       + [pltpu.VMEM((ATTN_CHUNK, ATTN_TQ), F32)] * TILES_PER_STEP,
        compiler_params=_params(3),
        name="moba_attn",
    )(qa_t, ka, va_t, kmean)


def _diff_kernel(q0_ref, q1_ref, k0_ref, k1_ref, vlo_ref, vhi_ref, lq1_ref, lk1_ref, lq2_ref, lk2_ref, subg_ref,
                 o_ref, acc_ref, *s_refs, lambda_init):
    sup = pl.program_id(2)
    q_refs = (q0_ref, q1_ref)
    k_refs = (k0_ref, k1_ref)
    chains = [(a, sub) for a in range(TILES_PER_STEP) for sub in range(2)]
    n_chains = len(chains)
    q_t = [q_refs[sub][:, a * ATTN_TQ:(a + 1) * ATTN_TQ] for a, sub in chains]
    causal_bias = _causal_bias()

    def scores_pass(n, start, n_blocks, causal_last):
        top = None
        for r in range(n_blocks):
            k_r = k_refs[chains[n][1]][pl.ds(start + r * ATTN_TQ, ATTN_TQ), :]
            s_r = jnp.dot(k_r, q_t[n], preferred_element_type=F32)
            if causal_last and r == n_blocks - 1:
                s_r = s_r + causal_bias
            s_refs[n][r * ATTN_TQ:(r + 1) * ATTN_TQ, :] = s_r
            part = jnp.max(s_r.reshape(ATTN_TQ // SUBLANES, SUBLANES, ATTN_TQ), axis=0)
            top = part if top is None else jnp.maximum(top, part)
        return jnp.max(top, axis=0, keepdims=True)

    def values_pass(n, start, rows, m_new, alpha):
        p_t = jnp.exp2(s_refs[n][:rows, :] - m_new).astype(BF16)
        v_t = jnp.concatenate([vlo_ref[:HEAD_DIM, pl.ds(start, rows)], vhi_ref[:, pl.ds(start, rows)]], axis=0)
        update = jnp.dot(v_t, p_t, preferred_element_type=F32)
        acc_ref[n] = update if alpha is None else alpha * acc_ref[n] + update

    own = pl.multiple_of(sup * ATTN_CHUNK, ATTN_CHUNK)
    m_chunk = scores_pass(0, own, 1, True)
    stats = []
    for n, (a, sub) in enumerate(chains):
        if n + 1 < n_chains:
            m_next = scores_pass(n + 1, own, chains[n + 1][0] + 1, True)
        else:
            m_next = scores_pass(0, 0, BLOCKS_PER_CHUNK, False)
        values_pass(n, own, (a + 1) * ATTN_TQ, m_chunk, None)
        stats.append(m_chunk)
        m_chunk = m_next

    def body(c, carry):
        stats, m_chunk = carry
        start = pl.multiple_of(c * ATTN_CHUNK, ATTN_CHUNK)
        out = []
        for n in range(n_chains):
            if n + 1 < n_chains:
                m_next = scores_pass(n + 1, start, BLOCKS_PER_CHUNK, False)
            else:
                m_next = scores_pass(0, pl.multiple_of((c + 1) * ATTN_CHUNK, ATTN_CHUNK), BLOCKS_PER_CHUNK, False)
            m_new = jnp.maximum(stats[n], m_chunk)
            values_pass(n, start, ATTN_CHUNK, m_new, jnp.exp2(stats[n] - m_new))
            out.append(m_new)
            m_chunk = m_next
        return tuple(out), m_chunk

    _past_chunks_loop(sup, body, (tuple(stats), m_chunk))

    lam = (jnp.exp(jnp.sum(lq1_ref[...] * lk1_ref[...], axis=1, keepdims=True))
           - jnp.exp(jnp.sum(lq2_ref[...] * lk2_ref[...], axis=1, keepdims=True)) + lambda_init)
    for a in range(TILES_PER_STEP):
        acc0, acc1 = acc_ref[2 * a], acc_ref[2 * a + 1]
        out = (acc0[:DIFF_V_DIM] * (1.0 / acc0[DIFF_V_DIM:DIFF_V_DIM + 1])
               - acc1[:DIFF_V_DIM] * (lam / acc1[DIFF_V_DIM:DIFF_V_DIM + 1]))
        inv = lax.rsqrt(jnp.mean(out * out, axis=0, keepdims=True) + NORM_EPS)
        out = out * inv * subg_ref[...]
        o_ref[a * ATTN_TQ:(a + 1) * ATTN_TQ, :] = (out * (1.0 - lambda_init)).T.astype(o_ref.dtype)


def _diff_attention(qb_t, kb, vb_t, lq1, lk1, lq2, lk2, sub_g_col, batch, seq, lambda_init):
    t = batch * seq
    n_steps = seq // ATTN_CHUNK
    q_spec = lambda c: pl.BlockSpec((None, HEAD_DIM, ATTN_CHUNK), lambda b, h, i: (2 * h + c, 0, b * n_steps + i))
    k_spec = lambda c: pl.BlockSpec((None, seq, HEAD_DIM), lambda b, h, i: (2 * h + c, b, 0))
    v_spec = lambda c: pl.BlockSpec((None, HEAD_DIM + ONES_ROWS, seq), lambda b, h, i: (2 * h + c, 0, b))
    vec_spec = pl.BlockSpec((1, HEAD_DIM), lambda b, h, i: (0, 0))
    return pl.pallas_call(
        functools.partial(_diff_kernel, lambda_init=lambda_init),
        grid=(batch, DIFF_HEADS, n_steps),
        in_specs=[
            q_spec(0), q_spec(1), k_spec(0), k_spec(1), v_spec(0), v_spec(1),
            vec_spec, vec_spec, vec_spec, vec_spec,
            pl.BlockSpec((DIFF_V_DIM, 1), lambda b, h, i: (0, 0)),
        ],
        out_specs=pl.BlockSpec((ATTN_CHUNK, DIFF_V_DIM), lambda b, h, i: (b * n_steps + i, h)),
        out_shape=jax.ShapeDtypeStruct((t, DIFF_HEADS * DIFF_V_DIM), BF16),
        scratch_shapes=[pltpu.VMEM((2 * TILES_PER_STEP, DIFF_V_DIM + ONES_ROWS, ATTN_TQ), F32)]
                       + [pltpu.VMEM((ATTN_CHUNK, ATTN_TQ), F32)] * (2 * TILES_PER_STEP),
        compiler_params=_params(3),
        name="diff_attn",
    )(qb_t, qb_t, kb, kb, vb_t, vb_t, lq1, lk1, lq2, lk2, sub_g_col)


def _mix_kernel(oa_ref, ob_ref, sga_ref, sgb_ref, x_ref, wm_ref, wd_ref, wo_ref, gpost_ref, gnext_ref,
                h_ref, u_ref):
    half = x_ref.shape[0] // 2
    for rows in (slice(0, half), slice(half, 2 * half)):
        ya = jnp.dot(oa_ref[rows, :], wm_ref[...], preferred_element_type=F32)
        yb = jnp.dot(ob_ref[rows, :], wd_ref[...], preferred_element_type=F32)
        mixed = sga_ref[rows, :].astype(F32) * ya + sgb_ref[rows, :].astype(F32) * yb
        z = jnp.dot(mixed.astype(BF16), wo_ref[...], preferred_element_type=F32)
        h = x_ref[rows, :] + z * _rms_scale(z) * gpost_ref[...]
        h_ref[rows, :] = h
        u_ref[rows, :] = (h * _rms_scale(h) * gnext_ref[...]).astype(BF16)


def _mix(oa, ob, gates, x2, wm, wd, wo, gpost, gnext, tm=512):
    t = x2.shape[0]
    const = lambda shape: pl.BlockSpec(shape, lambda i: (0, 0), pipeline_mode=pl.Buffered(1))
    row = lambda width, col=0: pl.BlockSpec((tm, width), lambda i: (i, col))
    return pl.pallas_call(
        _mix_kernel,
        grid=(t // tm,),
        in_specs=[
            row(SECTION), row(SECTION), row(D_MODEL, 0), row(D_MODEL, 1), row(D_MODEL),
            const((SECTION, D_MODEL)), const((SECTION, D_MODEL)), const((D_MODEL, D_MODEL)),
            const((1, D_MODEL)), const((1, D_MODEL)),
        ],
        out_specs=[row(D_MODEL), row(D_MODEL)],
        out_shape=[jax.ShapeDtypeStruct((t, D_MODEL), F32), jax.ShapeDtypeStruct((t, D_MODEL), BF16)],
        compiler_params=_params(1),
        name="mix_out",
    )(oa, ob, gates, gates, x2, wm, wd, wo, gpost, gnext)


def _mlp_kernel(u_ref, wup_ref, wdown_ref, o_ref):
    @pl.when(pl.program_id(1) == 0)
    def _():
        o_ref[...] = jnp.zeros_like(o_ref)

    a = jnp.dot(u_ref[...], wup_ref[...], preferred_element_type=F32)
    a = jnp.square(jnp.maximum(a, 0.0)).astype(BF16)
    o_ref[...] += jnp.dot(a, wdown_ref[...], preferred_element_type=F32)


def _mlp(u, wup, wdown, tm=1024, tf=1024):
    t = u.shape[0]
    return pl.pallas_call(
        _mlp_kernel,
        grid=(t // tm, D_FF // tf),
        in_specs=[
            pl.BlockSpec((tm, D_MODEL), lambda i, k: (i, 0)),
            pl.BlockSpec((D_MODEL, tf), lambda i, k: (0, k)),
            pl.BlockSpec((tf, D_MODEL), lambda i, k: (k, 0)),
        ],
        out_specs=pl.BlockSpec((tm, D_MODEL), lambda i, k: (i, 0)),
        out_shape=jax.ShapeDtypeStruct((t, D_MODEL), F32),
        compiler_params=_params(2),
        name="mlp",
    )(u, wup, wdown)


def _ple_kernel(h_ref, ff_ref, p_ref, gmlp_ref, gpre_ref, wgate_ref, wproj_ref, gpost_ref, o_ref):
    half = h_ref.shape[0] // 2
    halves = [slice(i * half, (i + 1) * half) for i in range(2)]
    normed = []
    for rows in halves:
        ff = ff_ref[rows, :]
        h = h_ref[rows, :] + ff * _rms_scale(ff) * gmlp_ref[...]
        normed.append((h, (h * _rms_scale(h) * gpre_ref[...]).astype(BF16)))
    for rows, (h, u) in zip(halves, normed):
        gate = jax.nn.sigmoid(jnp.dot(u, wgate_ref[...], preferred_element_type=F32))
        e = jnp.dot(p_ref[rows, :].astype(BF16), wproj_ref[...], preferred_element_type=F32) * gate
        o_ref[rows, :] = h + e * _rms_scale(e) * gpost_ref[...]


def _ple(h, ff, p2, gmlp, gpre, wgate, wproj, gpost, tm=512):
    t = h.shape[0]
    const = lambda shape: pl.BlockSpec(shape, lambda i: (0, 0), pipeline_mode=pl.Buffered(1))
    row = lambda width: pl.BlockSpec((tm, width), lambda i: (i, 0))
    return pl.pallas_call(
        _ple_kernel,
        grid=(t // tm,),
        in_specs=[
            row(D_MODEL), row(D_MODEL), row(PLE_DIM),
            const((1, D_MODEL)), const((1, D_MODEL)), const((D_MODEL, D_MODEL)), const((PLE_DIM, D_MODEL)),
            const((1, D_MODEL)),
        ],
        out_specs=row(D_MODEL),
        out_shape=jax.ShapeDtypeStruct((t, D_MODEL), F32),
        compiler_params=_params(1),
        name="ple",
    )(h, ff, p2, gmlp, gpre, wgate, wproj, gpost)


def _rope_tables(seq):
    half = ROT_DIM // 2
    inv_freq = 1.0 / (ROPE_THETA ** (np.arange(half, dtype=np.float64) * 2.0 / ROT_DIM))
    ang = np.arange(seq, dtype=np.float64)[:, None] * inv_freq[None, :]
    cos, sin = np.cos(ang), np.sin(ang)
    zeros = np.zeros((seq, half))
    rest = np.zeros((seq, HEAD_DIM - ROT_DIM))
    cos_t = np.concatenate([cos, cos, rest + 1.0], axis=1)
    sin_lo = np.concatenate([-sin, zeros, rest], axis=1)
    sin_hi = np.concatenate([zeros, sin, rest], axis=1)
    return tuple(jnp.asarray(t.astype(np.float32)) for t in (cos_t, sin_lo, sin_hi))


def kernel(x, p, w_in, w_br_moba, w_br_diff, w_out, lambda_q1, lambda_k1, lambda_q2, lambda_k2, diff_subln_g,
           g_mix_pre, g_mix_post, w_up, w_down, g_mlp_pre, g_mlp_post, w_ple_proj, w_ple_gate, g_ple_pre,
           g_ple_post):
    batch, seq, _ = x.shape
    depth = w_in.shape[0]
    t = batch * seq
    n_blocks = seq // MOBA_BLOCK
    cos_t, sin_lo, sin_hi = _rope_tables(seq)
    row = lambda v: v.reshape(1, -1).astype(F32)

    h = x.reshape(t, D_MODEL)
    for layer in range(depth):
        lambda_init = 0.8 - 0.6 * math.exp(-0.3 * layer)
        w_l = w_in[layer].astype(BF16)
        qa_t, ka, va_t, qb_t, kb, vb_t, kmean = _qkv_proj(h, row(g_mix_pre[layer]), w_l, cos_t, sin_lo, sin_hi, seq)
        gates = _gate_proj(h, row(g_mix_pre[layer]), w_l)
        kmean = kmean.transpose(1, 0, 2, 3).reshape(N_HEADS, batch * n_blocks, HEAD_DIM)
        oa = _moba_attention(qa_t, ka, va_t, kmean, batch, seq)
        ob = _diff_attention(qb_t, kb, vb_t, row(lambda_q1[layer]), row(lambda_k1[layer]), row(lambda_q2[layer]),
                             row(lambda_k2[layer]), diff_subln_g[layer].reshape(-1, 1).astype(F32),
                             batch, seq, lambda_init)
        h, u = _mix(oa, ob, gates, h, w_br_moba[layer].astype(BF16), w_br_diff[layer].astype(BF16),
                    w_out[layer].astype(BF16), row(g_mix_post[layer]), row(g_mlp_pre[layer]))
        ff = _mlp(u, w_up[layer].astype(BF16), w_down[layer].astype(BF16))
        h = _ple(h, ff, p[layer].reshape(t, PLE_DIM), row(g_mlp_post[layer]), row(g_ple_pre[layer]),
                 w_ple_gate[layer].astype(BF16), w_ple_proj[layer].astype(BF16), row(g_ple_post[layer]))
    return h.reshape(batch, seq, D_MODEL)
```

```python
import functools
import math

import jax
import jax.numpy as jnp
import numpy as np
from jax import lax
from jax.experimental import pallas as pl
from jax.experimental.pallas import tpu as pltpu

F32 = jnp.float32
BF16 = jnp.bfloat16

D_MODEL = 2048
HEAD_DIM = 128
N_HEADS = 8
MOBA_BLOCK = 256
MOBA_TOPK = 3
DIFF_HEADS = 4
DIFF_V_DIM = 2 * HEAD_DIM
ROT_DIM = HEAD_DIM // 4
ROPE_THETA = 500000.0
D_FF = 4 * D_MODEL
PLE_DIM = 256
NORM_EPS = 1e-6
SECTION = N_HEADS * HEAD_DIM
QKV_WIDTH = 6 * SECTION
GATE_WIDTH = 2 * D_MODEL

ATTN_TQ = 256
ATTN_CHUNK = 1024
BLOCKS_PER_CHUNK = ATTN_CHUNK // MOBA_BLOCK
TILES_PER_STEP = ATTN_CHUNK // ATTN_TQ
SUBLANES = 8
ONES_ROWS = 2 * SUBLANES
LOOP_UNROLLS = (4, 2, 1)

VMEM_LIMIT_BYTES = 56 * 1024 * 1024

NEG_INF = float("-inf")
Q_SCALE = HEAD_DIM ** -0.5 * math.log2(math.e)


def _params(n_axes):
    return pltpu.CompilerParams(dimension_semantics=("arbitrary",) * n_axes,
                                vmem_limit_bytes=VMEM_LIMIT_BYTES)


def _rms_scale(xf):
    return lax.rsqrt(jnp.mean(xf * xf, axis=-1, keepdims=True) + NORM_EPS)


def _normed_halves(x_ref, g_ref):
    half = x_ref.shape[0] // 2
    out = []
    for rows in (slice(0, half), slice(half, 2 * half)):
        xf = x_ref[rows, :]
        out.append((xf * _rms_scale(xf) * g_ref[...]).astype(BF16))
    return out


def _dot_halves(u_halves, w):
    return jnp.concatenate([jnp.dot(u, w, preferred_element_type=F32) for u in u_halves], axis=0)


def _qkv_kernel(x_ref, g_ref, w_ref, cos_ref, sin_lo_ref, sin_hi_ref,
                qa_ref, ka_ref, va_ref, qb_ref, kb_ref, vb_ref, kmean_ref, *, tm):
    u_halves = _normed_halves(x_ref, g_ref)
    cos, sin_lo, sin_hi = cos_ref[...], sin_lo_ref[...], sin_hi_ref[...]

    def section(j):
        y = _dot_halves(u_halves, w_ref[:, j * SECTION:(j + 1) * SECTION])
        return [y[:, h * HEAD_DIM:(h + 1) * HEAD_DIM] for h in range(N_HEADS)]

    def rope(yh):
        from_hi = pltpu.roll(yh, HEAD_DIM - ROT_DIM // 2, axis=1)
        from_lo = pltpu.roll(yh, ROT_DIM // 2, axis=1)
        return yh * cos + from_hi * sin_lo + from_lo * sin_hi

    for h, yh in enumerate(section(0)):
        qa_ref[h] = (rope(yh) * Q_SCALE).T.astype(BF16)
    for h, yh in enumerate(section(1)):
        kr = rope(yh)
        ka_ref[h] = kr.astype(BF16)
        kmean_ref[h] = jnp.mean(kr.reshape(tm // MOBA_BLOCK, MOBA_BLOCK, HEAD_DIM), axis=1)
    ones_tile = jnp.where(lax.broadcasted_iota(jnp.int32, (ONES_ROWS, tm), 0) == 0, 1.0, 0.0).astype(BF16)
    for h, yh in enumerate(section(2)):
        va_ref[h, :HEAD_DIM] = yh.T.astype(BF16)
        va_ref[h, HEAD_DIM:] = ones_tile
    for h, yh in enumerate(section(3)):
        qb_ref[h] = (rope(yh) * Q_SCALE).T.astype(BF16)
    for h, yh in enumerate(section(4)):
        kb_ref[h] = rope(yh).astype(BF16)
    for h, yh in enumerate(section(5)):
        vb_ref[h, :HEAD_DIM] = yh.T.astype(BF16)
        vb_ref[h, HEAD_DIM:] = ones_tile


def _qkv_proj(x2, g, w_all, cos_t, sin_lo, sin_hi, seq, tm=512):
    t = x2.shape[0]
    n_rows = t // tm
    pos_blocks = seq // tm
    row_major = jax.ShapeDtypeStruct((N_HEADS, t, HEAD_DIM), BF16)
    col_major = jax.ShapeDtypeStruct((N_HEADS, HEAD_DIM, t), BF16)
    row_spec = pl.BlockSpec((N_HEADS, tm, HEAD_DIM), lambda i: (0, i, 0))
    col_spec = pl.BlockSpec((N_HEADS, HEAD_DIM, tm), lambda i: (0, 0, i))
    tab_spec = pl.BlockSpec((tm, HEAD_DIM), lambda i: (i % pos_blocks, 0))
    val_major = jax.ShapeDtypeStruct((N_HEADS, HEAD_DIM + ONES_ROWS, t), BF16)
    val_spec = pl.BlockSpec((N_HEADS, HEAD_DIM + ONES_ROWS, tm), lambda i: (0, 0, i))
    return pl.pallas_call(
        functools.partial(_qkv_kernel, tm=tm),
        grid=(n_rows,),
        in_specs=[
            pl.BlockSpec((tm, D_MODEL), lambda i: (i, 0)),
            pl.BlockSpec((1, D_MODEL), lambda i: (0, 0)),
            pl.BlockSpec((D_MODEL, QKV_WIDTH), lambda i: (0, 0), pipeline_mode=pl.Buffered(1)),
            tab_spec, tab_spec, tab_spec,
        ],
        out_specs=[
            col_spec, row_spec, val_spec,
            col_spec, row_spec, val_spec,
            pl.BlockSpec((None, N_HEADS, tm // MOBA_BLOCK, HEAD_DIM), lambda i: (i, 0, 0, 0)),
        ],
        out_shape=[
            col_major, row_major, val_major,
            col_major, row_major, val_major,
            jax.ShapeDtypeStruct((n_rows, N_HEADS, tm // MOBA_BLOCK, HEAD_DIM), F32),
        ],
        compiler_params=_params(1),
        name="qkv_proj",
    )(x2, g, w_all, cos_t, sin_lo, sin_hi)


def _gate_kernel(x_ref, g_ref, wa_ref, wb_ref, o_ref, *, tn):
    u_halves = _normed_halves(x_ref, g_ref)
    for n, w_ref in enumerate((wa_ref, wb_ref)):
        for j in range(D_MODEL // tn):
            y = _dot_halves(u_halves, w_ref[:, j * tn:(j + 1) * tn])
            o_ref[:, n * D_MODEL + j * tn:n * D_MODEL + (j + 1) * tn] = jax.nn.sigmoid(y).astype(o_ref.dtype)


def _gate_proj(x2, g, w_all, tm=512, tn=1024):
    t = x2.shape[0]
    first = QKV_WIDTH // D_MODEL
    gate_w = lambda n: pl.BlockSpec((D_MODEL, D_MODEL), lambda i: (0, first + n), pipeline_mode=pl.Buffered(1))
    return pl.pallas_call(
        functools.partial(_gate_kernel, tn=tn),
        grid=(t // tm,),
        in_specs=[
            pl.BlockSpec((tm, D_MODEL), lambda i: (i, 0)),
            pl.BlockSpec((1, D_MODEL), lambda i: (0, 0)),
            gate_w(0), gate_w(1),
        ],
        out_specs=pl.BlockSpec((tm, GATE_WIDTH), lambda i: (i, 0)),
        out_shape=jax.ShapeDtypeStruct((t, GATE_WIDTH), BF16),
        compiler_params=_params(1),
        name="gate_proj",
    )(x2, g, w_all, w_all)


def _past_chunks_loop(n_chunks, body, carry):
    done = 0
    for width in LOOP_UNROLLS:
        def group(i, carry, width=width, base=done):
            for u in range(width):
                carry = body(base + width * i + u, carry)
            return carry

        trips = (n_chunks - done) // width
        carry = lax.fori_loop(0, trips, group, carry)
        done = done + trips * width
    return carry


def _causal_bias():
    key = lax.broadcasted_iota(jnp.int32, (ATTN_TQ, ATTN_TQ), 0)
    qry = lax.broadcasted_iota(jnp.int32, (ATTN_TQ, ATTN_TQ), 1)
    return jnp.where(key <= qry, 0.0, NEG_INF)


def _moba_kernel(q_ref, k_ref, v_ref, kmean_ref, o_ref, acc_ref, s0_ref, s1_ref, s2_ref, s3_ref, *, n_blocks):
    sup = pl.program_id(2)
    q_all = q_ref[...]

    km = kmean_ref[...]
    km_hi = km.astype(BF16)
    km_mid = (km - km_hi.astype(F32)).astype(BF16)
    km_lo = (km - km_hi.astype(F32) - km_mid.astype(F32)).astype(BF16)
    gate = (jnp.dot(km_lo, q_all, preferred_element_type=F32) + jnp.dot(km_mid, q_all, preferred_element_type=F32)
            + jnp.dot(km_hi, q_all, preferred_element_type=F32))
    blk = lax.broadcasted_iota(jnp.int32, gate.shape, 0).astype(F32)
    own_blk = (sup * TILES_PER_STEP
               + lax.broadcasted_iota(jnp.int32, (1, ATTN_CHUNK), 1) // MOBA_BLOCK).astype(F32)
    g = jnp.where(blk < own_blk, gate, NEG_INF)
    picks = []
    for _ in range(MOBA_TOPK):
        best = jnp.max(g, axis=0, keepdims=True)
        first = jnp.min(jnp.where(g == best, blk, float(n_blocks)), axis=0, keepdims=True)
        first = jnp.where(best > NEG_INF, first, -1.0)
        picks.append(first)
        g = jnp.where(blk == first, NEG_INF, g)

    def lanes(a):
        return slice(a * ATTN_TQ, (a + 1) * ATTN_TQ)

    q_t = [q_all[:, lanes(a)] for a in range(TILES_PER_STEP)]
    tile_picks = [[p[:, lanes(a)] for p in picks] for a in range(TILES_PER_STEP)]
    s_refs = (s0_ref, s1_ref, s2_ref, s3_ref)
    causal_bias = _causal_bias()

    def selection_bias(a, block_index):
        jf = jnp.asarray(block_index).astype(F32)
        chosen = (tile_picks[a][0] == jf) | (tile_picks[a][1] == jf) | (tile_picks[a][2] == jf)
        return jnp.where(chosen, 0.0, NEG_INF)

    def past_biases(a, c):
        return [selection_bias(a, c * BLOCKS_PER_CHUNK + r) for r in range(BLOCKS_PER_CHUNK)]

    def own_biases(a):
        return [selection_bias(a, sup * TILES_PER_STEP + r) for r in range(a)] + [causal_bias]

    def scores_pass(a, start, biases):
        top = None
        for r, bias in enumerate(biases):
            k_r = k_ref[pl.ds(start + r * MOBA_BLOCK, MOBA_BLOCK), :]
            s_r = jnp.dot(k_r, q_t[a], preferred_element_type=F32)
            if bias.shape[0] != 1:
                s_r = s_r + bias
            s_refs[a][r * MOBA_BLOCK:(r + 1) * MOBA_BLOCK, :] = s_r
            part = jnp.max(s_r.reshape(MOBA_BLOCK // SUBLANES, SUBLANES, ATTN_TQ), axis=0)
            if bias.shape[0] == 1:
                part = part + bias
            top = part if top is None else jnp.maximum(top, part)
        return jnp.max(top, axis=0, keepdims=True)

    def values_pass(a, start, biases, m_new, alpha):
        p_blocks = []
        for r, bias in enumerate(biases):
            shift = m_new - bias if bias.shape[0] == 1 else m_new
            p_blocks.append(jnp.exp2(s_refs[a][r * MOBA_BLOCK:(r + 1) * MOBA_BLOCK, :] - shift).astype(BF16))
        rows = len(biases) * MOBA_BLOCK
        update = jnp.dot(v_ref[:, pl.ds(start, rows)], jnp.concatenate(p_blocks, axis=0),
                         preferred_element_type=F32)
        acc_ref[a] = update if alpha is None else alpha * acc_ref[a] + update

    own = pl.multiple_of(sup * ATTN_CHUNK, ATTN_CHUNK)
    m_chunk = scores_pass(0, own, own_biases(0))
    tops = []
    for a in range(TILES_PER_STEP):
        if a + 1 < TILES_PER_STEP:
            m_next = scores_pass(a + 1, own, own_biases(a + 1))
        else:
            m_next = scores_pass(0, 0, past_biases(0, 0))
        values_pass(a, own, own_biases(a), m_chunk, None)
        tops.append(m_chunk)
        m_chunk = m_next

    def body(c, carry):
        tops, m_chunk = carry
        start = pl.multiple_of(c * ATTN_CHUNK, ATTN_CHUNK)
        out = []
        for a in range(TILES_PER_STEP):
            if a + 1 < TILES_PER_STEP:
                m_next = scores_pass(a + 1, start, past_biases(a + 1, c))
            else:
                m_next = scores_pass(0, pl.multiple_of((c + 1) * ATTN_CHUNK, ATTN_CHUNK), past_biases(0, c + 1))
            m_new = jnp.maximum(tops[a], m_chunk)
            values_pass(a, start, past_biases(a, c), m_new, jnp.exp2(tops[a] - m_new))
            out.append(m_new)
            m_chunk = m_next
        return tuple(out), m_chunk

    _past_chunks_loop(sup, body, (tuple(tops), m_chunk))
    for a in range(TILES_PER_STEP):
        acc = acc_ref[a]
        o_ref[lanes(a), :] = (acc[:HEAD_DIM] * (1.0 / acc[HEAD_DIM:HEAD_DIM + 1])).T.astype(o_ref.dtype)


def _moba_attention(qa_t, ka, va_t, kmean, batch, seq):
    t = batch * seq
    n_blocks = seq // MOBA_BLOCK
    n_steps = seq // ATTN_CHUNK
    return pl.pallas_call(
        functools.partial(_moba_kernel, n_blocks=n_blocks),
        grid=(batch, N_HEADS, n_steps),
        in_specs=[
            pl.BlockSpec((None, HEAD_DIM, ATTN_CHUNK), lambda b, h, i: (h, 0, b * n_steps + i)),
            pl.BlockSpec((None, seq, HEAD_DIM), lambda b, h, i: (h, b, 0)),
            pl.BlockSpec((None, HEAD_DIM + ONES_ROWS, seq), lambda b, h, i: (h, 0, b)),
            pl.BlockSpec((None, n_blocks, HEAD_DIM), lambda b, h, i: (h, b, 0)),
        ],
        out_specs=pl.BlockSpec((ATTN_CHUNK, HEAD_DIM), lambda b, h, i: (b * n_steps + i, h)),
        out_shape=jax.ShapeDtypeStruct((t, SECTION), BF16),
        scratch_shapes=[pltpu.VMEM((TILES_PER_STEP, HEAD_DIM + ONES_ROWS, ATTN_TQ), F32)]
                       + [pltpu.VMEM((ATTN_CHUNK, ATTN_TQ), F32)] * TILES_PER_STEP,
        compiler_params=_params(3),
        name="moba_attn",
    )(qa_t, ka, va_t, kmean)


def _diff_kernel(q0_ref, q1_ref, k0_ref, k1_ref, vlo_ref, vhi_ref, lq1_ref, lk1_ref, lq2_ref, lk2_ref, subg_ref,
                 o_ref, acc_ref, *s_refs, lambda_init):
    sup = pl.program_id(2)
    q_refs = (q0_ref, q1_ref)
    k_refs = (k0_ref, k1_ref)
    chains = [(a, sub) for a in range(TILES_PER_STEP) for sub in range(2)]
    n_chains = len(chains)
    q_t = [q_refs[sub][:, a * ATTN_TQ:(a + 1) * ATTN_TQ] for a, sub in chains]
    causal_bias = _causal_bias()

    def scores_pass(n, start, n_blocks, causal_last):
        top = None
        for r in range(n_blocks):
            k_r = k_refs[chains[n][1]][pl.ds(start + r * ATTN_TQ, ATTN_TQ), :]
            s_r = jnp.dot(k_r, q_t[n], preferred_element_type=F32)
            if causal_last and r == n_blocks - 1:
                s_r = s_r + causal_bias
            s_refs[n][r * ATTN_TQ:(r + 1) * ATTN_TQ, :] = s_r
            part = jnp.max(s_r.reshape(ATTN_TQ // SUBLANES, SUBLANES, ATTN_TQ), axis=0)
            top = part if top is None else jnp.maximum(top, part)
        return jnp.max(top, axis=0, keepdims=True)

    def values_pass(n, start, rows, m_new, alpha):
        p_t = jnp.exp2(s_refs[n][:rows, :] - m_new).astype(BF16)
        v_t = jnp.concatenate([vlo_ref[:HEAD_DIM, pl.ds(start, rows)], vhi_ref[:, pl.ds(start, rows)]], axis=0)
        update = jnp.dot(v_t, p_t, preferred_element_type=F32)
        acc_ref[n] = update if alpha is None else alpha * acc_ref[n] + update

    own = pl.multiple_of(sup * ATTN_CHUNK, ATTN_CHUNK)
    m_chunk = scores_pass(0, own, 1, True)
    stats = []
    for n, (a, sub) in enumerate(chains):
        if n + 1 < n_chains:
            m_next = scores_pass(n + 1, own, chains[n + 1][0] + 1, True)
        else:
            m_next = scores_pass(0, 0, BLOCKS_PER_CHUNK, False)
        values_pass(n, own, (a + 1) * ATTN_TQ, m_chunk, None)
        stats.append(m_chunk)
        m_chunk = m_next

    def body(c, carry):
        stats, m_chunk = carry
        start = pl.multiple_of(c * ATTN_CHUNK, ATTN_CHUNK)
        out = []
        for n in range(n_chains):
            if n + 1 < n_chains:
                m_next = scores_pass(n + 1, start, BLOCKS_PER_CHUNK, False)
            else:
                m_next = scores_pass(0, pl.multiple_of((c + 1) * ATTN_CHUNK, ATTN_CHUNK), BLOCKS_PER_CHUNK, False)
            m_new = jnp.maximum(stats[n], m_chunk)
            values_pass(n, start, ATTN_CHUNK, m_new, jnp.exp2(stats[n] - m_new))
            out.append(m_new)
            m_chunk = m_next
        return tuple(out), m_chunk

    _past_chunks_loop(sup, body, (tuple(stats), m_chunk))

    lam = (jnp.exp(jnp.sum(lq1_ref[...] * lk1_ref[...], axis=1, keepdims=True))
           - jnp.exp(jnp.sum(lq2_ref[...] * lk2_ref[...], axis=1, keepdims=True)) + lambda_init)
    for a in range(TILES_PER_STEP):
        acc0, acc1 = acc_ref[2 * a], acc_ref[2 * a + 1]
        out = (acc0[:DIFF_V_DIM] * (1.0 / acc0[DIFF_V_DIM:DIFF_V_DIM + 1])
               - acc1[:DIFF_V_DIM] * (lam / acc1[DIFF_V_DIM:DIFF_V_DIM + 1]))
        inv = lax.rsqrt(jnp.mean(out * out, axis=0, keepdims=True) + NORM_EPS)
        out = out * inv * subg_ref[...]
        o_ref[a * ATTN_TQ:(a + 1) * ATTN_TQ, :] = (out * (1.0 - lambda_init)).T.astype(o_ref.dtype)


def _diff_attention(qb_t, kb, vb_t, lq1, lk1, lq2, lk2, sub_g_col, batch, seq, lambda_init):
    t = batch * seq
    n_steps = seq // ATTN_CHUNK
    q_spec = lambda c: pl.BlockSpec((None, HEAD_DIM, ATTN_CHUNK), lambda b, h, i: (2 * h + c, 0, b * n_steps + i))
    k_spec = lambda c: pl.BlockSpec((None, seq, HEAD_DIM), lambda b, h, i: (2 * h + c, b, 0))
    v_spec = lambda c: pl.BlockSpec((None, HEAD_DIM + ONES_ROWS, seq), lambda b, h, i: (2 * h + c, 0, b))
    vec_spec = pl.BlockSpec((1, HEAD_DIM), lambda b, h, i: (0, 0))
    return pl.pallas_call(
        functools.partial(_diff_kernel, lambda_init=lambda_init),
        grid=(batch, DIFF_HEADS, n_steps),
        in_specs=[
            q_spec(0), q_spec(1), k_spec(0), k_spec(1), v_spec(0), v_spec(1),
            vec_spec, vec_spec, vec_spec, vec_spec,
            pl.BlockSpec((DIFF_V_DIM, 1), lambda b, h, i: (0, 0)),
        ],
        out_specs=pl.BlockSpec((ATTN_CHUNK, DIFF_V_DIM), lambda b, h, i: (b * n_steps + i, h)),
        out_shape=jax.ShapeDtypeStruct((t, DIFF_HEADS * DIFF_V_DIM), BF16),
        scratch_shapes=[pltpu.VMEM((2 * TILES_PER_STEP, DIFF_V_DIM + ONES_ROWS, ATTN_TQ), F32)]
                       + [pltpu.VMEM((ATTN_CHUNK, ATTN_TQ), F32)] * (2 * TILES_PER_STEP),
        compiler_params=_params(3),
        name="diff_attn",
    )(qb_t, qb_t, kb, kb, vb_t, vb_t, lq1, lk1, lq2, lk2, sub_g_col)


def _mix_kernel(oa_ref, ob_ref, sga_ref, sgb_ref, x_ref, wm_ref, wd_ref, wo_ref, gpost_ref, gnext_ref,
                h_ref, u_ref):
    half = x_ref.shape[0] // 2
    for rows in (slice(0, half), slice(half, 2 * half)):
        ya = jnp.dot(oa_ref[rows, :], wm_ref[...], preferred_element_type=F32)
        yb = jnp.dot(ob_ref[rows, :], wd_ref[...], preferred_element_type=F32)
        mixed = sga_ref[rows, :].astype(F32) * ya + sgb_ref[rows, :].astype(F32) * yb
        z = jnp.dot(mixed.astype(BF16), wo_ref[...], preferred_element_type=F32)
        h = x_ref[rows, :] + z * _rms_scale(z) * gpost_ref[...]
        h_ref[rows, :] = h
        u_ref[rows, :] = (h * _rms_scale(h) * gnext_ref[...]).astype(BF16)


def _mix(oa, ob, gates, x2, wm, wd, wo, gpost, gnext, tm=512):
    t = x2.shape[0]
    const = lambda shape: pl.BlockSpec(shape, lambda i: (0, 0), pipeline_mode=pl.Buffered(1))
    row = lambda width, col=0: pl.BlockSpec((tm, width), lambda i: (i, col))
    return pl.pallas_call(
        _mix_kernel,
        grid=(t // tm,),
        in_specs=[
            row(SECTION), row(SECTION), row(D_MODEL, 0), row(D_MODEL, 1), row(D_MODEL),
            const((SECTION, D_MODEL)), const((SECTION, D_MODEL)), const((D_MODEL, D_MODEL)),
            const((1, D_MODEL)), const((1, D_MODEL)),
        ],
        out_specs=[row(D_MODEL), row(D_MODEL)],
        out_shape=[jax.ShapeDtypeStruct((t, D_MODEL), F32), jax.ShapeDtypeStruct((t, D_MODEL), BF16)],
        compiler_params=_params(1),
        name="mix_out",
    )(oa, ob, gates, gates, x2, wm, wd, wo, gpost, gnext)


def _mlp_kernel(u_ref, wup_ref, wdown_ref, o_ref):
    @pl.when(pl.program_id(1) == 0)
    def _():
        o_ref[...] = jnp.zeros_like(o_ref)

    a = jnp.dot(u_ref[...], wup_ref[...], preferred_element_type=F32)
    a = jnp.square(jnp.maximum(a, 0.0)).astype(BF16)
    o_ref[...] += jnp.dot(a, wdown_ref[...], preferred_element_type=F32)


def _mlp(u, wup, wdown, tm=1024, tf=1024):
    t = u.shape[0]
    return pl.pallas_call(
        _mlp_kernel,
        grid=(t // tm, D_FF // tf),
        in_specs=[
            pl.BlockSpec((tm, D_MODEL), lambda i, k: (i, 0)),
            pl.BlockSpec((D_MODEL, tf), lambda i, k: (0, k)),
            pl.BlockSpec((tf, D_MODEL), lambda i, k: (k, 0)),
        ],
        out_specs=pl.BlockSpec((tm, D_MODEL), lambda i, k: (i, 0)),
        out_shape=jax.ShapeDtypeStruct((t, D_MODEL), F32),
        compiler_params=_params(2),
        name="mlp",
    )(u, wup, wdown)


def _ple_kernel(h_ref, ff_ref, p_ref, gmlp_ref, gpre_ref, wgate_ref, wproj_ref, gpost_ref, o_ref,
                h0_ref, g0_ref, e0_ref, h1_ref, g1_ref, e1_ref):
    i = pl.program_id(0)
    slots = ((h0_ref, g0_ref, e0_ref), (h1_ref, g1_ref, e1_ref))

    @pl.when(i == 0)
    def _():
        for ref in slots[1]:
            ref[...] = jnp.zeros_like(ref)

    def step(write, read):
        h_w, g_w, e_w = write
        h_r, g_r, e_r = read
        ff = ff_ref[...]
        h = h_ref[...] + ff * _rms_scale(ff) * gmlp_ref[...]
        h_w[...] = h
        e = e_r[...] * jax.nn.sigmoid(g_r[...])
        done = h_r[...] + e * _rms_scale(e) * gpost_ref[...]
        o_ref[...] = done
        u = (h * _rms_scale(h) * gpre_ref[...]).astype(BF16)
        g_w[...] = jnp.dot(u, wgate_ref[...], preferred_element_type=F32)
        bits = pltpu.bitcast(done, jnp.uint32).reshape(done.shape[0] // SUBLANES, SUBLANES, D_MODEL)
        folded = functools.reduce(jnp.bitwise_or, [bits[t] for t in range(bits.shape[0])])
        folded = functools.reduce(jnp.bitwise_or, [folded[:, j * PLE_DIM:(j + 1) * PLE_DIM]
                                                   for j in range(D_MODEL // PLE_DIM)])
        zero = pltpu.bitcast(lax.shift_right_logical(lax.shift_right_logical(folded, jnp.uint32(16)), jnp.uint32(16)), F32)
        p_in = p_ref[...]
        p_in = jnp.concatenate([p_in[:-SUBLANES], p_in[-SUBLANES:] + zero], axis=0).astype(BF16)
        e_w[...] = jnp.dot(p_in, wproj_ref[...], preferred_element_type=F32)

    @pl.when(i % 2 == 0)
    def _():
        step(slots[0], slots[1])

    @pl.when(i % 2 == 1)
    def _():
        step(slots[1], slots[0])


def _ple(h, ff, p2, gmlp, gpre, wgate, wproj, gpost, tm=256):
    t = h.shape[0]
    n_tiles = t // tm
    const = lambda shape: pl.BlockSpec(shape, lambda i: (0, 0), pipeline_mode=pl.Buffered(1))
    row_in = lambda width: pl.BlockSpec((tm, width), lambda i: (jnp.minimum(i, n_tiles - 1), 0))
    slot = [pltpu.VMEM((tm, D_MODEL), F32)] * 3
    return pl.pallas_call(
        _ple_kernel,
        grid=(n_tiles + 1,),
        in_specs=[
            row_in(D_MODEL), row_in(D_MODEL), row_in(PLE_DIM),
            const((1, D_MODEL)), const((1, D_MODEL)), const((D_MODEL, D_MODEL)), const((PLE_DIM, D_MODEL)),
            const((1, D_MODEL)),
        ],
        out_specs=pl.BlockSpec((tm, D_MODEL), lambda i: (jnp.maximum(i - 1, 0), 0)),
        out_shape=jax.ShapeDtypeStruct((t, D_MODEL), F32),
        scratch_shapes=slot + slot,
        compiler_params=_params(1),
        name="ple",
    )(h, ff, p2, gmlp, gpre, wgate, wproj, gpost)


def _rope_tables(seq):
    half = ROT_DIM // 2
    inv_freq = 1.0 / (ROPE_THETA ** (np.arange(half, dtype=np.float64) * 2.0 / ROT_DIM))
    ang = np.arange(seq, dtype=np.float64)[:, None] * inv_freq[None, :]
    cos, sin = np.cos(ang), np.sin(ang)
    zeros = np.zeros((seq, half))
    rest = np.zeros((seq, HEAD_DIM - ROT_DIM))
    cos_t = np.concatenate([cos, cos, rest + 1.0], axis=1)
    sin_lo = np.concatenate([-sin, zeros, rest], axis=1)
    sin_hi = np.concatenate([zeros, sin, rest], axis=1)
    return tuple(jnp.asarray(t.astype(np.float32)) for t in (cos_t, sin_lo, sin_hi))


def kernel(x, p, w_in, w_br_moba, w_br_diff, w_out, lambda_q1, lambda_k1, lambda_q2, lambda_k2, diff_subln_g,
           g_mix_pre, g_mix_post, w_up, w_down, g_mlp_pre, g_mlp_post, w_ple_proj, w_ple_gate, g_ple_pre,
           g_ple_post):
    batch, seq, _ = x.shape
    depth = w_in.shape[0]
    t = batch * seq
    n_blocks = seq // MOBA_BLOCK
    cos_t, sin_lo, sin_hi = _rope_tables(seq)
    row = lambda v: v.reshape(1, -1).astype(F32)

    h = x.reshape(t, D_MODEL)
    for layer in range(depth):
        lambda_init = 0.8 - 0.6 * math.exp(-0.3 * layer)
        w_l = w_in[layer].astype(BF16)
        qa_t, ka, va_t, qb_t, kb, vb_t, kmean = _qkv_proj(h, row(g_mix_pre[layer]), w_l, cos_t, sin_lo, sin_hi, seq)
        gates = _gate_proj(h, row(g_mix_pre[layer]), w_l)
        kmean = kmean.transpose(1, 0, 2, 3).reshape(N_HEADS, batch * n_blocks, HEAD_DIM)
        oa = _moba_attention(qa_t, ka, va_t, kmean, batch, seq)
        ob = _diff_attention(qb_t, kb, vb_t, row(lambda_q1[layer]), row(lambda_k1[layer]), row(lambda_q2[layer]),
                             row(lambda_k2[layer]), diff_subln_g[layer].reshape(-1, 1).astype(F32),
                             batch, seq, lambda_init)
        h, u = _mix(oa, ob, gates, h, w_br_moba[layer].astype(BF16), w_br_diff[layer].astype(BF16),
                    w_out[layer].astype(BF16), row(g_mix_post[layer]), row(g_mlp_pre[layer]))
        ff = _mlp(u, w_up[layer].astype(BF16), w_down[layer].astype(BF16))
        h = _ple(h, ff, p[layer].reshape(t, PLE_DIM), row(g_mlp_post[layer]), row(g_ple_pre[layer]),
                 w_ple_gate[layer].astype(BF16), w_ple_proj[layer].astype(BF16), row(g_ple_post[layer]))
    return h.reshape(batch, seq, D_MODEL)
```

```python
import functools
import math

import jax
import jax.numpy as jnp
import numpy as np
from jax import lax
from jax.experimental import pallas as pl
from jax.experimental.pallas import tpu as pltpu

F32 = jnp.float32
BF16 = jnp.bfloat16

D_MODEL = 2048
HEAD_DIM = 128
N_HEADS = 8
MOBA_BLOCK = 256
MOBA_TOPK = 3
DIFF_HEADS = 4
DIFF_V_DIM = 2 * HEAD_DIM
ROT_DIM = HEAD_DIM // 4
ROPE_THETA = 500000.0
D_FF = 4 * D_MODEL
PLE_DIM = 256
NORM_EPS = 1e-6
SECTION = N_HEADS * HEAD_DIM
QKV_WIDTH = 6 * SECTION
GATE_WIDTH = 2 * D_MODEL

ATTN_TQ = 256
ATTN_CHUNK = 1024
BLOCKS_PER_CHUNK = ATTN_CHUNK // MOBA_BLOCK
TILES_PER_STEP = ATTN_CHUNK // ATTN_TQ
SUBLANES = 8
ONES_ROWS = 2 * SUBLANES
LOOP_UNROLLS = (4, 2, 1)

VMEM_LIMIT_BYTES = 56 * 1024 * 1024

NEG_INF = float("-inf")
Q_SCALE = HEAD_DIM ** -0.5 * math.log2(math.e)


def _params(n_axes):
    return pltpu.CompilerParams(dimension_semantics=("arbitrary",) * n_axes,
                                vmem_limit_bytes=VMEM_LIMIT_BYTES)


def _rms_scale(xf):
    return lax.rsqrt(jnp.mean(xf * xf, axis=-1, keepdims=True) + NORM_EPS)


def _normed_halves(x_ref, g_ref):
    half = x_ref.shape[0] // 2
    out = []
    for rows in (slice(0, half), slice(half, 2 * half)):
        xf = x_ref[rows, :]
        out.append((xf * _rms_scale(xf) * g_ref[...]).astype(BF16))
    return out


def _dot_halves(u_halves, w):
    return jnp.concatenate([jnp.dot(u, w, preferred_element_type=F32) for u in u_halves], axis=0)


def _qkv_kernel(x_ref, g_ref, w_ref, cos_ref, sin_lo_ref, sin_hi_ref,
                qa_ref, ka_ref, va_ref, qb_ref, kb_ref, vb_ref, kmean_ref, *, tm):
    u_halves = _normed_halves(x_ref, g_ref)
    cos, sin_lo, sin_hi = cos_ref[...], sin_lo_ref[...], sin_hi_ref[...]

    def section(j):
        y = _dot_halves(u_halves, w_ref[:, j * SECTION:(j + 1) * SECTION])
        return [y[:, h * HEAD_DIM:(h + 1) * HEAD_DIM] for h in range(N_HEADS)]

    def rope(yh):
        from_hi = pltpu.roll(yh, HEAD_DIM - ROT_DIM // 2, axis=1)
        from_lo = pltpu.roll(yh, ROT_DIM // 2, axis=1)
        return yh * cos + from_hi * sin_lo + from_lo * sin_hi

    for h, yh in enumerate(section(0)):
        qa_ref[h] = (rope(yh) * Q_SCALE).T.astype(BF16)
    for h, yh in enumerate(section(1)):
        kr = rope(yh)
        ka_ref[h] = kr.astype(BF16)
        kmean_ref[h] = jnp.mean(kr.reshape(tm // MOBA_BLOCK, MOBA_BLOCK, HEAD_DIM), axis=1)
    ones_tile = jnp.where(lax.broadcasted_iota(jnp.int32, (ONES_ROWS, tm), 0) == 0, 1.0, 0.0).astype(BF16)
    for h, yh in enumerate(section(2)):
        va_ref[h, :HEAD_DIM] = yh.T.astype(BF16)
        va_ref[h, HEAD_DIM:] = ones_tile
    for h, yh in enumerate(section(3)):
        qb_ref[h] = (rope(yh) * Q_SCALE).T.astype(BF16)
    for h, yh in enumerate(section(4)):
        kb_ref[h] = rope(yh).astype(BF16)
    for h, yh in enumerate(section(5)):
        vb_ref[h, :HEAD_DIM] = yh.T.astype(BF16)
        vb_ref[h, HEAD_DIM:] = ones_tile


def _qkv_proj(x2, g, w_all, cos_t, sin_lo, sin_hi, seq, tm=512):
    t = x2.shape[0]
    n_rows = t // tm
    pos_blocks = seq // tm
    row_major = jax.ShapeDtypeStruct((N_HEADS, t, HEAD_DIM), BF16)
    col_major = jax.ShapeDtypeStruct((N_HEADS, HEAD_DIM, t), BF16)
    row_spec = pl.BlockSpec((N_HEADS, tm, HEAD_DIM), lambda i: (0, i, 0))
    col_spec = pl.BlockSpec((N_HEADS, HEAD_DIM, tm), lambda i: (0, 0, i))
    tab_spec = pl.BlockSpec((tm, HEAD_DIM), lambda i: (i % pos_blocks, 0))
    val_major = jax.ShapeDtypeStruct((N_HEADS, HEAD_DIM + ONES_ROWS, t), BF16)
    val_spec = pl.BlockSpec((N_HEADS, HEAD_DIM + ONES_ROWS, tm), lambda i: (0, 0, i))
    return pl.pallas_call(
        functools.partial(_qkv_kernel, tm=tm),
        grid=(n_rows,),
        in_specs=[
            pl.BlockSpec((tm, D_MODEL), lambda i: (i, 0)),
            pl.BlockSpec((1, D_MODEL), lambda i: (0, 0)),
            pl.BlockSpec((D_MODEL, QKV_WIDTH), lambda i: (0, 0), pipeline_mode=pl.Buffered(1)),
            tab_spec, tab_spec, tab_spec,
        ],
        out_specs=[
            col_spec, row_spec, val_spec,
            col_spec, row_spec, val_spec,
            pl.BlockSpec((None, N_HEADS, tm // MOBA_BLOCK, HEAD_DIM), lambda i: (i, 0, 0, 0)),
        ],
        out_shape=[
            col_major, row_major, val_major,
            col_major, row_major, val_major,
            jax.ShapeDtypeStruct((n_rows, N_HEADS, tm // MOBA_BLOCK, HEAD_DIM), F32),
        ],
        compiler_params=_params(1),
        name="qkv_proj",
    )(x2, g, w_all, cos_t, sin_lo, sin_hi)


def _gate_kernel(x_ref, g_ref, wa_ref, wb_ref, o_ref, *, tn):
    u_halves = _normed_halves(x_ref, g_ref)
    for n, w_ref in enumerate((wa_ref, wb_ref)):
        for j in range(D_MODEL // tn):
            y = _dot_halves(u_halves, w_ref[:, j * tn:(j + 1) * tn])
            o_ref[:, n * D_MODEL + j * tn:n * D_MODEL + (j + 1) * tn] = jax.nn.sigmoid(y).astype(o_ref.dtype)


def _gate_proj(x2, g, w_all, tm=512, tn=1024):
    t = x2.shape[0]
    first = QKV_WIDTH // D_MODEL
    gate_w = lambda n: pl.BlockSpec((D_MODEL, D_MODEL), lambda i: (0, first + n), pipeline_mode=pl.Buffered(1))
    return pl.pallas_call(
        functools.partial(_gate_kernel, tn=tn),
        grid=(t // tm,),
        in_specs=[
            pl.BlockSpec((tm, D_MODEL), lambda i: (i, 0)),
            pl.BlockSpec((1, D_MODEL), lambda i: (0, 0)),
            gate_w(0), gate_w(1),
        ],
        out_specs=pl.BlockSpec((tm, GATE_WIDTH), lambda i: (i, 0)),
        out_shape=jax.ShapeDtypeStruct((t, GATE_WIDTH), BF16),
        compiler_params=_params(1),
        name="gate_proj",
    )(x2, g, w_all, w_all)


def _past_chunks_loop(n_chunks, body, carry):
    done = 0
    for width in LOOP_UNROLLS:
        def group(i, carry, width=width, base=done):
            for u in range(width):
                carry = body(base + width * i + u, carry)
            return carry

        trips = (n_chunks - done) // width
        carry = lax.fori_loop(0, trips, group, carry)
        done = done + trips * width
    return carry


def _causal_bias():
    key = lax.broadcasted_iota(jnp.int32, (ATTN_TQ, ATTN_TQ), 0)
    qry = lax.broadcasted_iota(jnp.int32, (ATTN_TQ, ATTN_TQ), 1)
    return jnp.where(key <= qry, 0.0, NEG_INF)


def _moba_kernel(q_ref, k_ref, v_ref, kmean_ref, o_ref, acc_ref, s0_ref, s1_ref, s2_ref, s3_ref, *, n_blocks):
    sup = pl.program_id(2)
    q_all = q_ref[...]

    km = kmean_ref[...]
    km_hi = km.astype(BF16)
    km_mid = (km - km_hi.astype(F32)).astype(BF16)
    km_lo = (km - km_hi.astype(F32) - km_mid.astype(F32)).astype(BF16)
    gate = (jnp.dot(km_lo, q_all, preferred_element_type=F32) + jnp.dot(km_mid, q_all, preferred_element_type=F32)
            + jnp.dot(km_hi, q_all, preferred_element_type=F32))
    blk = lax.broadcasted_iota(jnp.int32, gate.shape, 0).astype(F32)
    own_blk = (sup * TILES_PER_STEP
               + lax.broadcasted_iota(jnp.int32, (1, ATTN_CHUNK), 1) // MOBA_BLOCK).astype(F32)
    g = jnp.where(blk < own_blk, gate, NEG_INF)
    picks = []
    for _ in range(MOBA_TOPK):
        best = jnp.max(g, axis=0, keepdims=True)
        first = jnp.min(jnp.where(g == best, blk, float(n_blocks)), axis=0, keepdims=True)
        first = jnp.where(best > NEG_INF, first, -1.0)
        picks.append(first)
        g = jnp.where(blk == first, NEG_INF, g)

    def lanes(a):
        return slice(a * ATTN_TQ, (a + 1) * ATTN_TQ)

    q_t = [q_all[:, lanes(a)] for a in range(TILES_PER_STEP)]
    tile_picks = [[p[:, lanes(a)] for p in picks] for a in range(TILES_PER_STEP)]
    s_refs = (s0_ref, s1_ref, s2_ref, s3_ref)
    causal_bias = _causal_bias()

    def selection_bias(a, block_index):
        jf = jnp.asarray(block_index).astype(F32)
        chosen = (tile_picks[a][0] == jf) | (tile_picks[a][1] == jf) | (tile_picks[a][2] == jf)
        return jnp.where(chosen, 0.0, NEG_INF)

    def past_biases(a, c):
        return [selection_bias(a, c * BLOCKS_PER_CHUNK + r) for r in range(BLOCKS_PER_CHUNK)]

    def own_biases(a):
        return [selection_bias(a, sup * TILES_PER_STEP + r) for r in range(a)] + [causal_bias]

    def scores_pass(a, start, biases):
        top = None
        for r, bias in enumerate(biases):
            k_r = k_ref[pl.ds(start + r * MOBA_BLOCK, MOBA_BLOCK), :]
            s_r = jnp.dot(k_r, q_t[a], preferred_element_type=F32)
            if bias.shape[0] != 1:
                s_r = s_r + bias
            s_refs[a][r * MOBA_BLOCK:(r + 1) * MOBA_BLOCK, :] = s_r
            part = jnp.max(s_r.reshape(MOBA_BLOCK // SUBLANES, SUBLANES, ATTN_TQ), axis=0)
            if bias.shape[0] == 1:
                part = part + bias
            top = part if top is None else jnp.maximum(top, part)
        return jnp.max(top, axis=0, keepdims=True)

    def values_pass(a, start, biases, m_new, alpha):
        p_blocks = []
        for r, bias in enumerate(biases):
            shift = m_new - bias if bias.shape[0] == 1 else m_new
            p_blocks.append(jnp.exp2(s_refs[a][r * MOBA_BLOCK:(r + 1) * MOBA_BLOCK, :] - shift).astype(BF16))
        rows = len(biases) * MOBA_BLOCK
        update = jnp.dot(v_ref[:, pl.ds(start, rows)], jnp.concatenate(p_blocks, axis=0),
                         preferred_element_type=F32)
        acc_ref[a] = update if alpha is None else alpha * acc_ref[a] + update

    own = pl.multiple_of(sup * ATTN_CHUNK, ATTN_CHUNK)
    m_chunk = scores_pass(0, own, own_biases(0))
    tops = []
    for a in range(TILES_PER_STEP):
        if a + 1 < TILES_PER_STEP:
            m_next = scores_pass(a + 1, own, own_biases(a + 1))
        else:
            m_next = scores_pass(0, 0, past_biases(0, 0))
        values_pass(a, own, own_biases(a), m_chunk, None)
        tops.append(m_chunk)
        m_chunk = m_next

    def body(c, carry):
        tops, m_chunk = carry
        start = pl.multiple_of(c * ATTN_CHUNK, ATTN_CHUNK)
        out = []
        for a in range(TILES_PER_STEP):
            if a + 1 < TILES_PER_STEP:
                m_next = scores_pass(a + 1, start, past_biases(a + 1, c))
            else:
                m_next = scores_pass(0, pl.multiple_of((c + 1) * ATTN_CHUNK, ATTN_CHUNK), past_biases(0, c + 1))
            m_new = jnp.maximum(tops[a], m_chunk)
            values_pass(a, start, past_biases(a, c), m_new, jnp.exp2(tops[a] - m_new))
            out.append(m_new)
            m_chunk = m_next
        return tuple(out), m_chunk

    _past_chunks_loop(sup, body, (tuple(tops), m_chunk))
    for a in range(TILES_PER_STEP):
        acc = acc_ref[a]
        o_ref[lanes(a), :] = (acc[:HEAD_DIM] * (1.0 / acc[HEAD_DIM:HEAD_DIM + 1])).T.astype(o_ref.dtype)


def _moba_attention(qa_t, ka, va_t, kmean, batch, seq):
    t = batch * seq
    n_blocks = seq // MOBA_BLOCK
    n_steps = seq // ATTN_CHUNK
    return pl.pallas_call(
        functools.partial(_moba_kernel, n_blocks=n_blocks),
        grid=(batch, N_HEADS, n_steps),
        in_specs=[
            pl.BlockSpec((None, HEAD_DIM, ATTN_CHUNK), lambda b, h, i: (h, 0, b * n_steps + i)),
            pl.BlockSpec((None, seq, HEAD_DIM), lambda b, h, i: (h, b, 0)),
            pl.BlockSpec((None, HEAD_DIM + ONES_ROWS, seq), lambda b, h, i: (h, 0, b)),
            pl.BlockSpec((None, n_blocks, HEAD_DIM), lambda b, h, i: (h, b, 0)),
        ],
        out_specs=pl.BlockSpec((ATTN_CHUNK, HEAD_DIM), lambda b, h, i: (b * n_steps + i, h)),
        out_shape=jax.ShapeDtypeStruct((t, SECTION), BF16),
        scratch_shapes=[pltpu.VMEM((TILES_PER_STEP, HEAD_DIM + ONES_ROWS, ATTN_TQ), F32)]
                       + [pltpu.VMEM((ATTN_CHUNK, ATTN_TQ), F32)] * TILES_PER_STEP,
        compiler_params=_params(3),
        name="moba_attn",
    )(qa_t, ka, va_t, kmean)


def _diff_kernel(q0_ref, q1_ref, k0_ref, k1_ref, vlo_ref, vhi_ref, lq1_ref, lk1_ref, lq2_ref, lk2_ref, subg_ref,
                 o_ref, acc_ref, *s_refs, lambda_init):
    sup = pl.program_id(2)
    q_refs = (q0_ref, q1_ref)
    k_refs = (k0_ref, k1_ref)
    chains = [(a, sub) for a in reversed(range(TILES_PER_STEP)) for sub in range(2)]
    chain_of = {c: n for n, c in enumerate(chains)}
    n_chains = len(chains)
    q_t = [q_refs[sub][:, a * ATTN_TQ:(a + 1) * ATTN_TQ] for a, sub in chains]
    causal_bias = _causal_bias()

    def scores_pass(n, start, n_blocks, causal_last):
        top = None
        for r in range(n_blocks):
            k_r = k_refs[chains[n][1]][pl.ds(start + r * ATTN_TQ, ATTN_TQ), :]
            s_r = jnp.dot(k_r, q_t[n], preferred_element_type=F32)
            if causal_last and r == n_blocks - 1:
                s_r = s_r + causal_bias
            s_refs[n][r * ATTN_TQ:(r + 1) * ATTN_TQ, :] = s_r
            part = jnp.max(s_r.reshape(ATTN_TQ // SUBLANES, SUBLANES, ATTN_TQ), axis=0)
            top = part if top is None else jnp.maximum(top, part)
        return jnp.max(top, axis=0, keepdims=True)

    def values_pass(n, start, rows, m_new, alpha):
        p_t = jnp.exp2(s_refs[n][:rows, :] - m_new).astype(BF16)
        v_t = jnp.concatenate([vlo_ref[:HEAD_DIM, pl.ds(start, rows)], vhi_ref[:, pl.ds(start, rows)]], axis=0)
        update = jnp.dot(v_t, p_t, preferred_element_type=F32)
        acc_ref[n] = update if alpha is None else alpha * acc_ref[n] + update

    own = pl.multiple_of(sup * ATTN_CHUNK, ATTN_CHUNK)
    m_chunk = scores_pass(0, own, chains[0][0] + 1, True)
    stats = []
    for n, (a, sub) in enumerate(chains):
        if n + 1 < n_chains:
            m_next = scores_pass(n + 1, own, chains[n + 1][0] + 1, True)
        else:
            m_next = scores_pass(0, 0, BLOCKS_PER_CHUNK, False)
        values_pass(n, own, (a + 1) * ATTN_TQ, m_chunk, None)
        stats.append(m_chunk)
        m_chunk = m_next

    def body(c, carry):
        stats, m_chunk = carry
        start = pl.multiple_of(c * ATTN_CHUNK, ATTN_CHUNK)
        out = []
        for n in range(n_chains):
            if n + 1 < n_chains:
                m_next = scores_pass(n + 1, start, BLOCKS_PER_CHUNK, False)
            else:
                m_next = scores_pass(0, pl.multiple_of((c + 1) * ATTN_CHUNK, ATTN_CHUNK), BLOCKS_PER_CHUNK, False)
            m_new = jnp.maximum(stats[n], m_chunk)
            values_pass(n, start, ATTN_CHUNK, m_new, jnp.exp2(stats[n] - m_new))
            out.append(m_new)
            m_chunk = m_next
        return tuple(out), m_chunk

    _past_chunks_loop(sup, body, (tuple(stats), m_chunk))

    lam = (jnp.exp(jnp.sum(lq1_ref[...] * lk1_ref[...], axis=1, keepdims=True))
           - jnp.exp(jnp.sum(lq2_ref[...] * lk2_ref[...], axis=1, keepdims=True)) + lambda_init)
    for a in range(TILES_PER_STEP):
        acc0, acc1 = acc_ref[chain_of[a, 0]], acc_ref[chain_of[a, 1]]
        out = (acc0[:DIFF_V_DIM] * (1.0 / acc0[DIFF_V_DIM:DIFF_V_DIM + 1])
               - acc1[:DIFF_V_DIM] * (lam / acc1[DIFF_V_DIM:DIFF_V_DIM + 1]))
        inv = lax.rsqrt(jnp.mean(out * out, axis=0, keepdims=True) + NORM_EPS)
        out = out * inv * subg_ref[...]
        o_ref[a * ATTN_TQ:(a + 1) * ATTN_TQ, :] = (out * (1.0 - lambda_init)).T.astype(o_ref.dtype)


def _diff_attention(qb_t, kb, vb_t, lq1, lk1, lq2, lk2, sub_g_col, batch, seq, lambda_init):
    t = batch * seq
    n_steps = seq // ATTN_CHUNK
    q_spec = lambda c: pl.BlockSpec((None, HEAD_DIM, ATTN_CHUNK), lambda b, h, i: (2 * h + c, 0, b * n_steps + i))
    k_spec = lambda c: pl.BlockSpec((None, seq, HEAD_DIM), lambda b, h, i: (2 * h + c, b, 0))
    v_spec = lambda c: pl.BlockSpec((None, HEAD_DIM + ONES_ROWS, seq), lambda b, h, i: (2 * h + c, 0, b))
    vec_spec = pl.BlockSpec((1, HEAD_DIM), lambda b, h, i: (0, 0))
    return pl.pallas_call(
        functools.partial(_diff_kernel, lambda_init=lambda_init),
        grid=(batch, DIFF_HEADS, n_steps),
        in_specs=[
            q_spec(0), q_spec(1), k_spec(0), k_spec(1), v_spec(0), v_spec(1),
            vec_spec, vec_spec, vec_spec, vec_spec,
            pl.BlockSpec((DIFF_V_DIM, 1), lambda b, h, i: (0, 0)),
        ],
        out_specs=pl.BlockSpec((ATTN_CHUNK, DIFF_V_DIM), lambda b, h, i: (b * n_steps + i, h)),
        out_shape=jax.ShapeDtypeStruct((t, DIFF_HEADS * DIFF_V_DIM), BF16),
        scratch_shapes=[pltpu.VMEM((2 * TILES_PER_STEP, DIFF_V_DIM + ONES_ROWS, ATTN_TQ), F32)]
                       + [pltpu.VMEM((ATTN_CHUNK, ATTN_TQ), F32)] * (2 * TILES_PER_STEP),
        compiler_params=_params(3),
        name="diff_attn",
    )(qb_t, qb_t, kb, kb, vb_t, vb_t, lq1, lk1, lq2, lk2, sub_g_col)


def _mix_kernel(oa_ref, ob_ref, sga_ref, sgb_ref, x_ref, wm_ref, wd_ref, wo_ref, gpost_ref, gnext_ref,
                h_ref, u_ref):
    half = x_ref.shape[0] // 2
    for rows in (slice(0, half), slice(half, 2 * half)):
        ya = jnp.dot(oa_ref[rows, :], wm_ref[...], preferred_element_type=F32)
        yb = jnp.dot(ob_ref[rows, :], wd_ref[...], preferred_element_type=F32)
        mixed = sga_ref[rows, :].astype(F32) * ya + sgb_ref[rows, :].astype(F32) * yb
        z = jnp.dot(mixed.astype(BF16), wo_ref[...], preferred_element_type=F32)
        h = x_ref[rows, :] + z * _rms_scale(z) * gpost_ref[...]
        h_ref[rows, :] = h
        u_ref[rows, :] = (h * _rms_scale(h) * gnext_ref[...]).astype(BF16)


def _mix(oa, ob, gates, x2, wm, wd, wo, gpost, gnext, tm=512):
    t = x2.shape[0]
    const = lambda shape: pl.BlockSpec(shape, lambda i: (0, 0), pipeline_mode=pl.Buffered(1))
    row = lambda width, col=0: pl.BlockSpec((tm, width), lambda i: (i, col))
    return pl.pallas_call(
        _mix_kernel,
        grid=(t // tm,),
        in_specs=[
            row(SECTION), row(SECTION), row(D_MODEL, 0), row(D_MODEL, 1), row(D_MODEL),
            const((SECTION, D_MODEL)), const((SECTION, D_MODEL)), const((D_MODEL, D_MODEL)),
            const((1, D_MODEL)), const((1, D_MODEL)),
        ],
        out_specs=[row(D_MODEL), row(D_MODEL)],
        out_shape=[jax.ShapeDtypeStruct((t, D_MODEL), F32), jax.ShapeDtypeStruct((t, D_MODEL), BF16)],
        compiler_params=_params(1),
        name="mix_out",
    )(oa, ob, gates, gates, x2, wm, wd, wo, gpost, gnext)


def _mlp_kernel(u_ref, wup_ref, wdown_ref, o_ref):
    @pl.when(pl.program_id(1) == 0)
    def _():
        o_ref[...] = jnp.zeros_like(o_ref)

    a = jnp.dot(u_ref[...], wup_ref[...], preferred_element_type=F32)
    a = jnp.square(jnp.maximum(a, 0.0)).astype(BF16)
    o_ref[...] += jnp.dot(a, wdown_ref[...], preferred_element_type=F32)


def _mlp(u, wup, wdown, tm=1024, tf=1024):
    t = u.shape[0]
    return pl.pallas_call(
        _mlp_kernel,
        grid=(t // tm, D_FF // tf),
        in_specs=[
            pl.BlockSpec((tm, D_MODEL), lambda i, k: (i, 0)),
            pl.BlockSpec((D_MODEL, tf), lambda i, k: (0, k)),
            pl.BlockSpec((tf, D_MODEL), lambda i, k: (k, 0)),
        ],
        out_specs=pl.BlockSpec((tm, D_MODEL), lambda i, k: (i, 0)),
        out_shape=jax.ShapeDtypeStruct((t, D_MODEL), F32),
        compiler_params=_params(2),
        name="mlp",
    )(u, wup, wdown)


def _ple_kernel(h_ref, ff_ref, p_ref, gmlp_ref, gpre_ref, wgate_ref, wproj_ref, gpost_ref, o_ref):
    half = h_ref.shape[0] // 2
    halves = [slice(i * half, (i + 1) * half) for i in range(2)]
    normed = []
    for rows in halves:
        ff = ff_ref[rows, :]
        h = h_ref[rows, :] + ff * _rms_scale(ff) * gmlp_ref[...]
        normed.append((h, (h * _rms_scale(h) * gpre_ref[...]).astype(BF16)))
    for rows, (h, u) in zip(halves, normed):
        gate = jax.nn.sigmoid(jnp.dot(u, wgate_ref[...], preferred_element_type=F32))
        e = jnp.dot(p_ref[rows, :].astype(BF16), wproj_ref[...], preferred_element_type=F32) * gate
        o_ref[rows, :] = h + e * _rms_scale(e) * gpost_ref[...]


def _ple(h, ff, p2, gmlp, gpre, wgate, wproj, gpost, tm=512):
    t = h.shape[0]
    const = lambda shape: pl.BlockSpec(shape, lambda i: (0, 0), pipeline_mode=pl.Buffered(1))
    row = lambda width: pl.BlockSpec((tm, width), lambda i: (i, 0))
    return pl.pallas_call(
        _ple_kernel,
        grid=(t // tm,),
        in_specs=[
            row(D_MODEL), row(D_MODEL), row(PLE_DIM),
            const((1, D_MODEL)), const((1, D_MODEL)), const((D_MODEL, D_MODEL)), const((PLE_DIM, D_MODEL)),
            const((1, D_MODEL)),
        ],
        out_specs=row(D_MODEL),
        out_shape=jax.ShapeDtypeStruct((t, D_MODEL), F32),
        compiler_params=_params(1),
        name="ple",
    )(h, ff, p2, gmlp, gpre, wgate, wproj, gpost)


def _rope_tables(seq):
    half = ROT_DIM // 2
    inv_freq = 1.0 / (ROPE_THETA ** (np.arange(half, dtype=np.float64) * 2.0 / ROT_DIM))
    ang = np.arange(seq, dtype=np.float64)[:, None] * inv_freq[None, :]
    cos, sin = np.cos(ang), np.sin(ang)
    zeros = np.zeros((seq, half))
    rest = np.zeros((seq, HEAD_DIM - ROT_DIM))
    cos_t = np.concatenate([cos, cos, rest + 1.0], axis=1)
    sin_lo = np.concatenate([-sin, zeros, rest], axis=1)
    sin_hi = np.concatenate([zeros, sin, rest], axis=1)
    return tuple(jnp.asarray(t.astype(np.float32)) for t in (cos_t, sin_lo, sin_hi))


def kernel(x, p, w_in, w_br_moba, w_br_diff, w_out, lambda_q1, lambda_k1, lambda_q2, lambda_k2, diff_subln_g,
           g_mix_pre, g_mix_post, w_up, w_down, g_mlp_pre, g_mlp_post, w_ple_proj, w_ple_gate, g_ple_pre,
           g_ple_post):
    batch, seq, _ = x.shape
    depth = w_in.shape[0]
    t = batch * seq
    n_blocks = seq // MOBA_BLOCK
    cos_t, sin_lo, sin_hi = _rope_tables(seq)
    row = lambda v: v.reshape(1, -1).astype(F32)

    h = x.reshape(t, D_MODEL)
    for layer in range(depth):
        lambda_init = 0.8 - 0.6 * math.exp(-0.3 * layer)
        w_l = w_in[layer].astype(BF16)
        qa_t, ka, va_t, qb_t, kb, vb_t, kmean = _qkv_proj(h, row(g_mix_pre[layer]), w_l, cos_t, sin_lo, sin_hi, seq)
        gates = _gate_proj(h, row(g_mix_pre[layer]), w_l)
        kmean = kmean.transpose(1, 0, 2, 3).reshape(N_HEADS, batch * n_blocks, HEAD_DIM)
        oa = _moba_attention(qa_t, ka, va_t, kmean, batch, seq)
        ob = _diff_attention(qb_t, kb, vb_t, row(lambda_q1[layer]), row(lambda_k1[layer]), row(lambda_q2[layer]),
                             row(lambda_k2[layer]), diff_subln_g[layer].reshape(-1, 1).astype(F32),
                             batch, seq, lambda_init)
        h, u = _mix(oa, ob, gates, h, w_br_moba[layer].astype(BF16), w_br_diff[layer].astype(BF16),
                    w_out[layer].astype(BF16), row(g_mix_post[layer]), row(g_mlp_pre[layer]))
        ff = _mlp(u, w_up[layer].astype(BF16), w_down[layer].astype(BF16))
        h = _ple(h, ff, p[layer].reshape(t, PLE_DIM), row(g_mlp_post[layer]), row(g_ple_pre[layer]),
                 w_ple_gate[layer].astype(BF16), w_ple_proj[layer].astype(BF16), row(g_ple_post[layer]))
    return h.reshape(batch, seq, D_MODEL)
```
